```python
import jax
import jax.numpy as jnp
from jax import lax
import numpy as np


D_MODEL = 2048
BATCH = 8
SEQ = 4096
DEPTH = 4

N_MIXERS = 3
NSA_HEADS = 16
NSA_KV_GROUPS = 4
NSA_HEAD_DIM = 128
NSA_CMP_BLOCK = 32
NSA_CMP_STRIDE = 16
NSA_SEL_BLOCK = 64
NSA_TOPK = 16
NSA_WINDOW = 512
NSA_Q_CHUNK = 16
NSA_PROJ = NSA_HEADS * NSA_HEAD_DIM + 6 * NSA_KV_GROUPS * NSA_HEAD_DIM + 3 * NSA_HEADS
HGRN_HEADS = 16
HGRN_HEAD_DIM = D_MODEL // HGRN_HEADS
HGRN_CHUNK = 64
HGRN_PROJ = 3 * HGRN_HEADS * HGRN_HEAD_DIM + D_MODEL
SWA_HEADS = 32
SWA_KV_GROUPS = 4
SWA_HEAD_DIM = 64
SWA_WINDOW = 128
SWA_PROJ = SWA_HEADS * SWA_HEAD_DIM + 2 * SWA_KV_GROUPS * SWA_HEAD_DIM
BAND_BLOCK = 128
D_FF = 5632
N_EXPERTS = 8
TOP_K = 2
N_ADA = 6
RMS_EPS = 1e-6
NEG_BIG = -1e30
SEL_FORCE = 1e9
N_NSA_LAYERS = (DEPTH + 2) // 3
N_HGRN_LAYERS = (DEPTH + 1) // 3
N_SWA_LAYERS = DEPTH // 3
N_DENSE_LAYERS = (DEPTH + 1) // 2
N_MOE_LAYERS = DEPTH // 2

kernel_name = 'hybrid_nsa_hgrn2_swa_moe_adaln'

F32 = jnp.float32


def rms_norm(x, g):
    xf = x.astype(F32)
    y = xf * lax.rsqrt(jnp.mean(xf * xf, axis=-1, keepdims=True) + RMS_EPS)
    return (y * g.astype(F32)).astype(x.dtype)


def alibi_slopes(n_heads):
    return jnp.exp2(-8.0 * jnp.arange(1, n_heads + 1, dtype=F32) / n_heads)


def masked_softmax(s, valid, sink=None):
    s = jnp.where(valid, s, NEG_BIG)
    m = jnp.max(s, axis=-1, keepdims=True)
    if sink is not None:
        m = jnp.maximum(m, sink)
    e = jnp.where(valid, jnp.exp(s - m), 0.0)
    den = jnp.sum(e, axis=-1, keepdims=True)
    if sink is not None:
        den = den + jnp.exp(sink - m)
    return e / jnp.maximum(den, 1e-30)


def banded_attention(q, k, v, window, slopes, sinks=None):
    B, T, G, R, d = q.shape
    nb = T // BAND_BLOCK
    span = BAND_BLOCK + window
    kp = jnp.pad(k, ((0, 0), (window, 0), (0, 0), (0, 0)))
    vp = jnp.pad(v, ((0, 0), (window, 0), (0, 0), (0, 0)))
    dist = jnp.arange(BAND_BLOCK)[:, None] + window - jnp.arange(span)[None, :]
    in_band = (dist >= 0) & (dist < window)
    bias = -slopes[:, :, None, None] * dist.astype(F32)
    sink = None if sinks is None else sinks.astype(F32)[None, :, :, None, None]
    scale = d ** -0.5

    def block(n):
        t0 = n * BAND_BLOCK
        qb = lax.dynamic_slice_in_dim(q, t0, BAND_BLOCK, axis=1)
        kb = lax.dynamic_slice_in_dim(kp, t0, span, axis=1)
        vb = lax.dynamic_slice_in_dim(vp, t0, span, axis=1)
        s = jnp.einsum('bqgrd,bkgd->bgrqk', qb, kb).astype(F32) * scale + bias
        key_pos = t0 - window + jnp.arange(span)
        valid = in_band & (key_pos >= 0)[None, :]
        p = masked_softmax(s, valid, sink)
        return jnp.einsum('bgrqk,bkgd->bqgrd', p.astype(v.dtype), vb)

    out = lax.map(block, jnp.arange(nb))
    return jnp.moveaxis(out, 0, 1).reshape(B, T, G, R, d)


def nsa_mixer(h, w_in, w_out, q_gain, k_gain, cmp_pos, cmp_w1, cmp_w2):
    B, T, _ = h.shape
    H, G, dh = NSA_HEADS, NSA_KV_GROUPS, NSA_HEAD_DIM
    R = H // G
    kv = G * dh
    cuts = [H * dh + j * kv for j in range(7)]
    q, kc, vc, ks, vs, kw, vw, gl = jnp.split(h @ w_in, cuts, axis=-1)
    q = rms_norm(q.reshape(B, T, G, R, dh), q_gain)
    kc = kc.reshape(B, T, G, dh)
    vc = vc.reshape(B, T, G, dh)
    ks = rms_norm(ks.reshape(B, T, G, dh), k_gain[1])
    vs = vs.reshape(B, T, G, dh)
    kw = rms_norm(kw.reshape(B, T, G, dh), k_gain[2])
    vw = vw.reshape(B, T, G, dh)
    slopes = alibi_slopes(H).reshape(G, R)
    scale = dh ** -0.5

    n_cmp = (T - NSA_CMP_BLOCK) // NSA_CMP_STRIDE + 1
    starts = jnp.arange(n_cmp) * NSA_CMP_STRIDE
    tok = starts[:, None] + jnp.arange(NSA_CMP_BLOCK)[None, :]

    def compress(a, j):
        blocks = a[:, tok] + cmp_pos[j][None, None, :, None, :]
        hid = jax.nn.gelu(jnp.einsum('bnlgd,lde->bnge', blocks, cmp_w1[j]))
        return hid @ cmp_w2[j]

    k_cmp = rms_norm(compress(kc, 0), k_gain[0])
    v_cmp = compress(vc, 1)
    cmp_end = starts + NSA_CMP_BLOCK - 1

    n_blk = T // NSA_SEL_BLOCK
    top_n = min(NSA_TOPK, n_blk)
    blk_start = jnp.arange(n_blk) * NSA_SEL_BLOCK
    overlap = ((starts[:, None] < blk_start[None, :] + NSA_SEL_BLOCK)
               & (starts[:, None] + NSA_CMP_BLOCK > blk_start[None, :])).astype(F32)
    ks_blk = ks.reshape(B, n_blk, NSA_SEL_BLOCK, G, dh).transpose(0, 3, 1, 2, 4)
    vs_blk = vs.reshape(B, n_blk, NSA_SEL_BLOCK, G, dh).transpose(0, 3, 1, 2, 4)
    gather = jax.vmap(jax.vmap(lambda xb, ib: xb[ib]))
    blk_ids = jnp.arange(n_blk)

    def chunk(n):
        t0 = n * NSA_Q_CHUNK
        qc = lax.dynamic_slice_in_dim(q, t0, NSA_Q_CHUNK, axis=1)
        t = t0 + jnp.arange(NSA_Q_CHUNK)
        s = (jnp.einsum('bqgrd,bngd->bgrqn', qc, k_cmp).astype(F32) * scale
             - slopes[:, :, None, None] * (t[:, None] - cmp_end[None, :]).astype(F32))
        p_cmp = masked_softmax(s, cmp_end[None, :] <= t[:, None])
        o_cmp = jnp.einsum('bgrqn,bngd->bqgrd', p_cmp.astype(h.dtype), v_cmp)
        imp = jnp.einsum('bgrqn,nj->bgqj', p_cmp, overlap)
        cur = t // NSA_SEL_BLOCK
        forced = ((blk_ids[None, :] == 0) | (blk_ids[None, :] == cur[:, None])
                  | (blk_ids[None, :] == cur[:, None] - 1))
        future = blk_start[None, :] > t[:, None]
        score = jnp.where(future, -SEL_FORCE, jnp.where(forced, SEL_FORCE, imp))
        _, idx = lax.top_k(score, top_n)
        ksel = gather(ks_blk, idx)
        vsel = gather(vs_blk, idx)
        pos = idx[..., None] * NSA_SEL_BLOCK + jnp.arange(NSA_SEL_BLOCK)
        dist = t[None, None, :, None, None] - pos
        s = (jnp.einsum('bqgrd,bgqnld->bgrqnl', qc, ksel).astype(F32) * scale
             - slopes[None, :, :, None, None, None] * dist[:, :, None].astype(F32))
        valid = jnp.broadcast_to((dist >= 0)[:, :, None], s.shape)
        p = masked_softmax(s.reshape(B, G, R, NSA_Q_CHUNK, -1), valid.reshape(B, G, R, NSA_Q_CHUNK, -1))
        o_slc = jnp.einsum('bgrqm,bgqmd->bqgrd', p.astype(h.dtype), vsel.reshape(B, G, NSA_Q_CHUNK, -1, dh))
        return o_cmp, o_slc

    o_cmp, o_slc = lax.map(chunk, jnp.arange(T // NSA_Q_CHUNK))
    o_cmp = jnp.moveaxis(o_cmp, 0, 1).reshape(B, T, G, R, dh)
    o_slc = jnp.moveaxis(o_slc, 0, 1).reshape(B, T, G, R, dh)
    o_win = banded_attention(q, kw, vw, NSA_WINDOW, slopes)
    gates = jax.nn.sigmoid(gl.astype(F32)).reshape(B, T, G, R, 3).astype(h.dtype)
    o = gates[..., 0:1] * o_cmp + gates[..., 1:2] * o_slc + gates[..., 2:3] * o_win
    return o.reshape(B, T, H * dh) @ w_out


def hgrn2_mixer(h, w_in, w_out, o_gain, lower_bound):
    B, T, _ = h.shape
    H, dk = HGRN_HEADS, HGRN_HEAD_DIM
    hd = H * dk
    q, fz, iv, g = jnp.split(h @ w_in, [hd, 2 * hd, 3 * hd], axis=-1)
    q = jax.nn.silu(q.astype(F32)).reshape(B, T, H, dk)
    lb = lower_bound.reshape(H, dk)
    logf = jnp.logaddexp(jnp.log(lb), jnp.log1p(-lb) + jax.nn.log_sigmoid(fz.astype(F32).reshape(B, T, H, dk)))
    k = -jnp.expm1(logf)
    v = iv.astype(F32).reshape(B, T, H, dk)
    C = HGRN_CHUNK
    nc = T // C

    def to_chunks(a):
        return a.reshape(B, nc, C, H, dk).transpose(1, 0, 3, 2, 4)

    causal = jnp.tril(jnp.ones((C, C), dtype=bool))

    def step(S, xs):
        qc, kc, vc, lf = xs
        b = jnp.cumsum(lf, axis=2)
        diff = b[:, :, :, None, :] - b[:, :, None, :, :]
        decay = jnp.exp(jnp.where(causal[:, :, None], diff, -jnp.inf))
        attn = jnp.einsum('bhtk,bhsk,bhtsk->bhts', qc, kc, decay)
        o = jnp.einsum('bhts,bhsv->bhtv', attn, vc) + jnp.einsum('bhtk,bhkv->bhtv', qc * jnp.exp(b), S)
        b_last = b[:, :, -1:, :]
        S = jnp.exp(b_last[:, :, 0, :])[..., None] * S + jnp.einsum('bhsk,bhsv->bhkv', kc * jnp.exp(b_last - b), vc)
        return S, o

    S0 = jnp.zeros((B, H, dk, dk), F32)
    _, o = lax.scan(step, S0, (to_chunks(q), to_chunks(k), to_chunks(v), to_chunks(logf)))
    o = o.transpose(1, 0, 3, 2, 4).reshape(B, T, H, dk)
    o = rms_norm(o, o_gain) * jax.nn.silu(g.astype(F32)).reshape(B, T, H, dk)
    return o.reshape(B, T, hd).astype(h.dtype) @ w_out


def swa_mixer(h, w_in, w_out, q_gain, k_gain, sinks):
    B, T, _ = h.shape
    H, G, dh = SWA_HEADS, SWA_KV_GROUPS, SWA_HEAD_DIM
    R = H // G
    q, k, v = jnp.split(h @ w_in, [H * dh, H * dh + G * dh], axis=-1)
    q = rms_norm(q.reshape(B, T, G, R, dh), q_gain)
    k = rms_norm(k.reshape(B, T, G, dh), k_gain)
    v = v.reshape(B, T, G, dh)
    o = banded_attention(q, k, v, SWA_WINDOW, alibi_slopes(H).reshape(G, R), sinks.reshape(G, R))
    return o.reshape(B, T, H * dh) @ w_out


def swiglu(h, w_gu, w_down):
    g, u = jnp.split(h @ w_gu, 2, axis=-1)
    return (jax.nn.silu(g) * u) @ w_down


def moe_ffn(h, router, w_gu, w_down):
    B, T, D = h.shape
    hf = h.reshape(B * T, D)
    logits = (hf @ router).astype(F32)
    top_val, top_idx = lax.top_k(logits, TOP_K)
    top_w = jax.nn.softmax(top_val, axis=-1)
    combine = jnp.sum(jax.nn.one_hot(top_idx, N_EXPERTS, dtype=F32) * top_w[..., None], axis=1)
    y = jnp.zeros_like(hf)
    for e in range(N_EXPERTS):
        y = y + combine[:, e:e + 1].astype(h.dtype) * swiglu(hf, w_gu[e], w_down[e])
    return y.reshape(B, T, D)


def setup_inputs(seed: int = 0) -> dict:
    key = jax.random.key(seed)

    def normal(i, shape, scale):
        return scale * jax.random.normal(jax.random.fold_in(key, i), shape, F32)

    def gain(i, shape):
        return 1.0 + 0.02 * jax.random.normal(jax.random.fold_in(key, i), shape, F32)

    D = D_MODEL
    return {
        'x': normal(0, (BATCH, SEQ, D), 1.0),
        'c': normal(1, (BATCH, D), 1.0),
        'norm_mix_g': gain(2, (DEPTH, D)),
        'norm_ffn_g': gain(3, (DEPTH, D)),
        'ada_w': normal(4, (DEPTH, D, N_ADA * D), D ** -0.5),
        'ada_b': normal(5, (DEPTH, N_ADA * D), 0.02),
        'nsa_w_in': normal(6, (N_NSA_LAYERS, D, NSA_PROJ), D ** -0.5),
        'nsa_w_out': normal(7, (N_NSA_LAYERS, NSA_HEADS * NSA_HEAD_DIM, D), (NSA_HEADS * NSA_HEAD_DIM) ** -0.5),
        'nsa_q_gain': gain(8, (N_NSA_LAYERS, NSA_HEAD_DIM)),
        'nsa_k_gain': gain(9, (N_NSA_LAYERS, 3, NSA_HEAD_DIM)),
        'nsa_cmp_pos': normal(10, (N_NSA_LAYERS, 2, NSA_CMP_BLOCK, NSA_HEAD_DIM), 0.1),
        'nsa_cmp_w1': normal(11, (N_NSA_LAYERS, 2, NSA_CMP_BLOCK, NSA_HEAD_DIM, NSA_HEAD_DIM), (NSA_CMP_BLOCK * NSA_HEAD_DIM) ** -0.5),
        'nsa_cmp_w2': normal(12, (N_NSA_LAYERS, 2, NSA_HEAD_DIM, NSA_HEAD_DIM), NSA_HEAD_DIM ** -0.5),
        'hgrn_w_in': normal(13, (N_HGRN_LAYERS, D, HGRN_PROJ), D ** -0.5),
        'hgrn_w_out': normal(14, (N_HGRN_LAYERS, HGRN_HEADS * HGRN_HEAD_DIM, D), (HGRN_HEADS * HGRN_HEAD_DIM) ** -0.5),
        'hgrn_o_gain': gain(15, (N_HGRN_LAYERS, HGRN_HEAD_DIM)),
        'hgrn_lb_logits': normal(16, (DEPTH, HGRN_HEADS * HGRN_HEAD_DIM), 1.0),
        'swa_w_in': normal(17, (N_SWA_LAYERS, D, SWA_PROJ), D ** -0.5),
        'swa_w_out': normal(18, (N_SWA_LAYERS, SWA_HEADS * SWA_HEAD_DIM, D), (SWA_HEADS * SWA_HEAD_DIM) ** -0.5),
        'swa_q_gain': gain(19, (N_SWA_LAYERS, SWA_HEAD_DIM)),
        'swa_k_gain': gain(20, (N_SWA_LAYERS, SWA_HEAD_DIM)),
        'swa_sinks': normal(21, (N_SWA_LAYERS, SWA_HEADS), 1.0),
        'ffn_w_gu': normal(22, (N_DENSE_LAYERS, D, 2 * D_FF), D ** -0.5),
        'ffn_w_down': normal(23, (N_DENSE_LAYERS, D_FF, D), D_FF ** -0.5),
        'moe_router': normal(24, (N_MOE_LAYERS, D, N_EXPERTS), D ** -0.5),
        'moe_w_gu': normal(25, (N_MOE_LAYERS, N_EXPERTS, D, 2 * D_FF), D ** -0.5),
        'moe_w_down': normal(26, (N_MOE_LAYERS, N_EXPERTS, D_FF, D), D_FF ** -0.5),
    }


def reference(x, c, norm_mix_g, norm_ffn_g, ada_w, ada_b, nsa_w_in, nsa_w_out, nsa_q_gain, nsa_k_gain,
              nsa_cmp_pos, nsa_cmp_w1, nsa_cmp_w2, hgrn_w_in, hgrn_w_out, hgrn_o_gain, hgrn_lb_logits,
              swa_w_in, swa_w_out, swa_q_gain, swa_k_gain, swa_sinks, ffn_w_gu, ffn_w_down,
              moe_router, moe_w_gu, moe_w_down):
    cond = jax.nn.silu(c)
    lb_all = jax.nn.softmax(hgrn_lb_logits.astype(F32), axis=0)
    n_a = n_b = n_c = n_d = n_m = 0
    for i in range(DEPTH):
        mod = (cond @ ada_w[i] + ada_b[i])[:, None, :]
        sh1, sc1, g1, sh2, sc2, g2 = jnp.split(mod, N_ADA, axis=-1)
        h = rms_norm(x, norm_mix_g[i]) * (1.0 + sc1) + sh1
        kind = i % N_MIXERS
        if kind == 0:
            y = nsa_mixer(h, nsa_w_in[n_a], nsa_w_out[n_a], nsa_q_gain[n_a], nsa_k_gain[n_a],
                          nsa_cmp_pos[n_a], nsa_cmp_w1[n_a], nsa_cmp_w2[n_a])
            n_a += 1
        elif kind == 1:
            lb = jnp.sum(lb_all[1:i + 1], axis=0)
            y = hgrn2_mixer(h, hgrn_w_in[n_b], hgrn_w_out[n_b], hgrn_o_gain[n_b], lb)
            n_b += 1
        else:
            y = swa_mixer(h, swa_w_in[n_c], swa_w_out[n_c], swa_q_gain[n_c], swa_k_gain[n_c], swa_sinks[n_c])
            n_c += 1
        x = x + g1 * y
        h = rms_norm(x, norm_ffn_g[i]) * (1.0 + sc2) + sh2
        if i % 2 == 0:
            y = swiglu(h, ffn_w_gu[n_d], ffn_w_down[n_d])
            n_d += 1
        else:
            y = moe_ffn(h, moe_router[n_m], moe_w_gu[n_m], moe_w_down[n_m])
            n_m += 1
        x = x + g2 * y
    return x
```

```python
import functools

import jax
import jax.numpy as jnp
from jax import lax
from jax.experimental import pallas as pl
from jax.experimental.pallas import tpu as pltpu

F32 = jnp.float32
BF16 = jnp.bfloat16
I32 = jnp.int32

RMS_EPS = 1e-6
NEG_BIG = -1e30
SEL_FORCE = 1e9
N_ADA = 6

LANES = 128
SUBLANES = 8
VMEM_BYTES_V7X = 64 * 1024 * 1024

NSA_HEADS, NSA_GROUPS, NSA_DH = 16, 4, 128
NSA_REP = NSA_HEADS // NSA_GROUPS
NSA_CMP_BLOCK, NSA_CMP_STRIDE, NSA_SEL_BLOCK, NSA_TOPK, NSA_WINDOW = 32, 16, 64, 16, 512
HGRN_HEADS, HGRN_BLOCK = 16, 16
SWA_HEADS, SWA_GROUPS, SWA_DH, SWA_WINDOW = 32, 4, 64, 128
SWA_REP = SWA_HEADS // SWA_GROUPS
N_EXPERTS, TOP_K = 8, 2

_NT = (((1,), (1,)), ((), ()))
_TN = (((0,), (0,)), ((), ()))


def _params(sem, vmem_mb):
    assert vmem_mb * 1024 * 1024 < VMEM_BYTES_V7X
    return pltpu.CompilerParams(dimension_semantics=sem, vmem_limit_bytes=vmem_mb * 1024 * 1024)


def _rms(x, gain):
    return x * lax.rsqrt(jnp.mean(x * x, axis=-1, keepdims=True) + RMS_EPS) * gain


def _silu(x):
    return x * jax.nn.sigmoid(x)


def _ada_kernel(c_ref, w_ref, b_ref, o_ref):
    cond = _silu(c_ref[...])
    o_ref[0] = jnp.dot(cond, w_ref[0], preferred_element_type=F32,
                       precision=lax.Precision.HIGHEST) + b_ref[0]


def ada_all(c, ada_w, ada_b):
    depth, d, n = ada_w.shape
    b = c.shape[0]
    tn = 1024
    return pl.pallas_call(
        _ada_kernel,
        grid=(depth, n // tn),
        in_specs=[pl.BlockSpec((b, d), lambda l, j: (0, 0)),
                  pl.BlockSpec((1, d, tn), lambda l, j: (l, 0, j)),
                  pl.BlockSpec((1, 1, tn), lambda l, j: (l, 0, j))],
        out_specs=pl.BlockSpec((1, b, tn), lambda l, j: (l, 0, j)),
        out_shape=jax.ShapeDtypeStruct((depth, b, n), F32),
        compiler_params=_params(("arbitrary", "arbitrary"), 40),
        name="ada",
    )(c, ada_w, ada_b.reshape(depth, 1, n))


def _norm_mod_kernel(x_ref, g_ref, sc_ref, sh_ref, o_ref):
    y = _rms(x_ref[0], g_ref[...])
    o_ref[0] = (y * (1.0 + sc_ref[0]) + sh_ref[0]).astype(o_ref.dtype)


def norm_mod(x, gain, mod, sc_idx, sh_idx, out_dtype):
    b, t, d = x.shape
    tm = 512
    return pl.pallas_call(
        _norm_mod_kernel,
        grid=(b, t // tm),
        in_specs=[pl.BlockSpec((1, tm, d), lambda i, j: (i, j, 0)),
                  pl.BlockSpec((1, d), lambda i, j: (0, 0)),
                  pl.BlockSpec((1, 1, d), lambda i, j: (i * N_ADA + sc_idx, 0, 0)),
                  pl.BlockSpec((1, 1, d), lambda i, j: (i * N_ADA + sh_idx, 0, 0))],
        out_specs=pl.BlockSpec((1, tm, d), lambda i, j: (i, j, 0)),
        out_shape=jax.ShapeDtypeStruct((b, t, d), out_dtype),
        compiler_params=_params(("parallel", "parallel"), 40),
        name="norm_mod",
    )(x, gain.reshape(1, d), mod, mod)


def _mm_kernel(a_ref, w_ref, o_ref):
    o_ref[...] = jnp.dot(a_ref[...], w_ref[...], preferred_element_type=F32).astype(o_ref.dtype)


def _mm_res_kernel(a_ref, w_ref, x_ref, g_ref, o_ref):
    y = jnp.dot(a_ref[...], w_ref[...], preferred_element_type=F32)
    o_ref[...] = x_ref[...] + g_ref[0] * y


def matmul(a, w, out_dtype, tn=512):
    m, k = a.shape
    n = w.shape[1]
    tm = min(1024, m)
    return pl.pallas_call(
        _mm_kernel,
        grid=(m // tm, n // tn),
        in_specs=[pl.BlockSpec((tm, k), lambda i, j: (i, 0)),
                  pl.BlockSpec((k, tn), lambda i, j: (0, j))],
        out_specs=pl.BlockSpec((tm, tn), lambda i, j: (i, j)),
        out_shape=jax.ShapeDtypeStruct((m, n), out_dtype),
        compiler_params=_params(("parallel", "arbitrary"), 40),
        name="matmul",
    )(a, w)


def matmul_residual(a, w, x, mod, gate_idx, seq):
    m, k = a.shape
    n = w.shape[1]
    tm = min(1024, seq)
    tn = 512
    per_seq = seq // tm
    return pl.pallas_call(
        _mm_res_kernel,
        grid=(m // tm, n // tn),
        in_specs=[pl.BlockSpec((tm, k), lambda i, j: (i, 0)),
                  pl.BlockSpec((k, tn), lambda i, j: (0, j)),
                  pl.BlockSpec((tm, tn), lambda i, j: (i, j)),
                  pl.BlockSpec((1, 1, tn), lambda i, j: ((i // per_seq) * N_ADA + gate_idx, 0, j))],
        out_specs=pl.BlockSpec((tm, tn), lambda i, j: (i, j)),
        out_shape=jax.ShapeDtypeStruct((m, n), F32),
        compiler_params=_params(("parallel", "arbitrary"), 40),
        name="matmul_residual",
    )(a, w, x, mod)


def _ffn_step(h_bf16, wg_ref, wu_ref, wd_ref):
    g = jnp.dot(h_bf16, wg_ref[...], preferred_element_type=F32)
    u = jnp.dot(h_bf16, wu_ref[...], preferred_element_type=F32)
    a = (_silu(g) * u).astype(BF16)
    return jnp.dot(a, wd_ref[...], preferred_element_type=F32)


def _ffn_dense_kernel(h_ref, wg_ref, wu_ref, wd_ref, x_ref, g_ref, o_ref, acc_ref, *, nf):
    f = pl.program_id(1)
    part = _ffn_step(h_ref[...], wg_ref, wu_ref, wd_ref)

    @pl.when(f == 0)
    def _():
        acc_ref[...] = part

    @pl.when(f > 0)
    def _():
        acc_ref[...] += part

    @pl.when(f == nf - 1)
    def _():
        o_ref[...] = x_ref[...] + g_ref[0] * acc_ref[...]


def ffn_dense(h, w_gu, w_down, x, mod, gate_idx, seq):
    m, d = h.shape
    dff = w_down.shape[0]
    tm, tf = 512, 512
    nf = dff // tf
    per_seq = seq // tm
    return pl.pallas_call(
        functools.partial(_ffn_dense_kernel, nf=nf),
        grid=(m // tm, nf),
        in_specs=[pl.BlockSpec((tm, d), lambda i, f: (i, 0)),
                  pl.BlockSpec((d, tf), lambda i, f: (0, f)),
                  pl.BlockSpec((d, tf), lambda i, f: (0, nf + f)),
                  pl.BlockSpec((tf, d), lambda i, f: (f, 0)),
                  pl.BlockSpec((tm, d), lambda i, f: (i, 0)),
                  pl.BlockSpec((1, 1, d), lambda i, f: ((i // per_seq) * N_ADA + gate_idx, 0, 0))],
        out_specs=pl.BlockSpec((tm, d), lambda i, f: (i, 0)),
        out_shape=jax.ShapeDtypeStruct((m, d), F32),
        scratch_shapes=[pltpu.VMEM((tm, d), F32)],
        compiler_params=_params(("parallel", "arbitrary"), 48),
        name="ffn_dense",
    )(h, w_gu, w_gu, w_down, x, mod)


def _ffn_grouped_kernel(te_ref, nu_ref, h_ref, wg_ref, wu_ref, wd_ref, o_ref, acc_ref, hb_ref, *, nf):
    i = pl.program_id(0)
    f = pl.program_id(1)
    used = i < nu_ref[0]

    @pl.when(used & (f == 0))
    def _():
        hb_ref[...] = h_ref[...].astype(BF16)

    @pl.when(used)
    def _():
        part = _ffn_step(hb_ref[...], wg_ref, wu_ref, wd_ref)

        @pl.when(f == 0)
        def _():
            acc_ref[...] = part

        @pl.when(f > 0)
        def _():
            acc_ref[...] += part

    @pl.when(f == nf - 1)
    def _():
        o_ref[...] = jnp.where(used, acc_ref[...], 0.0)


MOE_TILE = 512


def ffn_grouped(hg, w_gu, w_down, tile_expert, n_used):
    mp, d = hg.shape
    dff = w_down.shape[1]
    tm, tf = MOE_TILE, 512
    nf = dff // tf

    def fsel(i, f, nu):
        return jnp.where(i < nu[0], f, 0)

    grid_spec = pltpu.PrefetchScalarGridSpec(
        num_scalar_prefetch=2,
        grid=(mp // tm, nf),
        in_specs=[pl.BlockSpec((tm, d), lambda i, f, te, nu: (i, 0)),
                  pl.BlockSpec((None, d, tf), lambda i, f, te, nu: (te[i], 0, fsel(i, f, nu))),
                  pl.BlockSpec((None, d, tf), lambda i, f, te, nu: (te[i], 0, nf + fsel(i, f, nu))),
                  pl.BlockSpec((None, tf, d), lambda i, f, te, nu: (te[i], fsel(i, f, nu), 0))],
        out_specs=pl.BlockSpec((tm, d), lambda i, f, te, nu: (i, 0)),
        scratch_shapes=[pltpu.VMEM((tm, d), F32), pltpu.VMEM((tm, d), BF16)],
    )
    return pl.pallas_call(
        functools.partial(_ffn_grouped_kernel, nf=nf),
        grid_spec=grid_spec,
        out_shape=jax.ShapeDtypeStruct((mp, d), F32),
        compiler_params=_params(("arbitrary", "arbitrary"), 48),
        name="ffn_grouped",
    )(tile_expert, n_used, hg, w_gu, w_gu, w_down)


def _split3(x):
    hi = x.astype(BF16)
    r1 = x - hi.astype(F32)
    mid = r1.astype(BF16)
    lo = (r1 - mid.astype(F32)).astype(BF16)
    return hi, mid, lo


def _route_kernel(h_ref, r_ref, meta_ref, cnt_ref, carry_ref, *, tr):
    i = pl.program_id(0)

    @pl.when(i == 0)
    def _():
        carry_ref[...] = jnp.zeros_like(carry_ref)

    logits = jnp.dot(h_ref[...], r_ref[...], preferred_element_type=F32,
                     precision=lax.Precision.HIGHEST)
    lane = lax.broadcasted_iota(I32, (tr, LANES), 1).astype(F32)
    logits = jnp.where(lane < N_EXPERTS, logits, -jnp.inf)
    m1 = jnp.max(logits, axis=-1, keepdims=True)
    e1 = jnp.min(jnp.where(logits == m1, lane, float(LANES)), axis=-1, keepdims=True)
    oh1 = lane == e1
    rest = jnp.where(oh1, -jnp.inf, logits)
    m2 = jnp.max(rest, axis=-1, keepdims=True)
    e2 = jnp.min(jnp.where(rest == m2, lane, float(LANES)), axis=-1, keepdims=True)
    oh2 = lane == e2
    ex = jnp.exp(m2 - m1)
    w1 = 1.0 / (1.0 + ex)
    w2 = ex / (1.0 + ex)
    oh = jnp.where(oh1 | oh2, 1.0, 0.0)
    row = lax.broadcasted_iota(I32, (tr, tr), 0)
    col = lax.broadcasted_iota(I32, (tr, tr), 1)
    strict_lower = jnp.where(col < row, 1.0, 0.0).astype(BF16)
    prefix = jnp.dot(strict_lower, oh.astype(BF16), preferred_element_type=F32) + carry_ref[...]
    r1 = jnp.sum(jnp.where(oh1, prefix, 0.0), axis=-1, keepdims=True)
    r2 = jnp.sum(jnp.where(oh2, prefix, 0.0), axis=-1, keepdims=True)
    total = carry_ref[...] + jnp.sum(oh, axis=0, keepdims=True)
    carry_ref[...] = total
    cnt_ref[...] = total
    meta = jnp.where(lane == 0, e1, jnp.where(lane == 1, e2, jnp.where(lane == 2, w1, jnp.where(
        lane == 3, w2, jnp.where(lane == 4, r1, jnp.where(lane == 5, r2, 0.0))))))
    meta_ref[...] = meta


def moe_route(h, router):
    n, d = h.shape
    tr = min(1024, n)
    rp = jnp.zeros((d, LANES), F32).at[:, :N_EXPERTS].set(router)
    return pl.pallas_call(
        functools.partial(_route_kernel, tr=tr),
        grid=(n // tr,),
        in_specs=[pl.BlockSpec((tr, d), lambda i: (i, 0)),
                  pl.BlockSpec((d, LANES), lambda i: (0, 0))],
        out_specs=[pl.BlockSpec((tr, LANES), lambda i: (i, 0)),
                   pl.BlockSpec((1, LANES), lambda i: (0, 0))],
        out_shape=[jax.ShapeDtypeStruct((n, LANES), F32), jax.ShapeDtypeStruct((1, LANES), F32)],
        scratch_shapes=[pltpu.VMEM((1, LANES), F32)],
        compiler_params=_params(("arbitrary",), 48),
        name="moe_route",
    )(h, rp)


def _row_copy(src_hbm, src_row, dst, dst_row, sem):
    return pltpu.make_async_copy(src_hbm.at[pl.ds(src_row, 1), :], dst.at[pl.ds(dst_row, 1), :], sem)


def _scatter_kernel(pos_ref, h_hbm, init_hbm, o_hbm, sem, *, ts):
    del init_hbm
    base = pl.program_id(0) * ts

    def start(t, c):
        for k in range(TOP_K):
            _row_copy(h_hbm, base + t, o_hbm, pos_ref[TOP_K * t + k], sem).start()
        return c

    def wait(t, c):
        for k in range(TOP_K):
            _row_copy(h_hbm, base + t, o_hbm, pos_ref[TOP_K * t + k], sem).wait()
        return c

    lax.fori_loop(0, ts, start, 0)
    lax.fori_loop(0, ts, wait, 0)


def moe_scatter(h, pos_flat, mp):
    n, d = h.shape
    ts = min(1024, n)
    return pl.pallas_call(
        functools.partial(_scatter_kernel, ts=ts),
        grid=(n // ts,),
        in_specs=[pl.BlockSpec((TOP_K * ts,), lambda i: (i,), memory_space=pltpu.SMEM),
                  pl.BlockSpec(memory_space=pl.ANY),
                  pl.BlockSpec(memory_space=pl.ANY)],
        out_specs=pl.BlockSpec(memory_space=pl.ANY),
        out_shape=jax.ShapeDtypeStruct((mp, d), F32),
        scratch_shapes=[pltpu.SemaphoreType.DMA(())],
        input_output_aliases={2: 0},
        compiler_params=_params(("arbitrary",), 32),
        name="moe_scatter",
    )(pos_flat, h, jnp.zeros((mp, d), F32))


def _combine_kernel(pos_ref, meta_ref, x_ref, g_ref, y_hbm, o_ref, rows_ref, sem, *, tc):
    def start(t, c):
        for k in range(TOP_K):
            _row_copy(y_hbm, pos_ref[TOP_K * t + k], rows_ref.at[k], t, sem).start()
        return c

    def wait(t, c):
        for k in range(TOP_K):
            _row_copy(y_hbm, pos_ref[TOP_K * t + k], rows_ref.at[k], t, sem).wait()
        return c

    lax.fori_loop(0, tc, start, 0)
    lax.fori_loop(0, tc, wait, 0)
    meta = meta_ref[...]
    y = meta[:, 2:3] * rows_ref[0] + meta[:, 3:4] * rows_ref[1]
    o_ref[...] = x_ref[...] + g_ref[0] * y


def moe_combine(yg, pos_flat, meta, x, mod, gate_idx, seq):
    n, d = x.shape
    tc = 256
    per_seq = seq // tc
    return pl.pallas_call(
        functools.partial(_combine_kernel, tc=tc),
        grid=(n // tc,),
        in_specs=[pl.BlockSpec((TOP_K * tc,), lambda i: (i,), memory_space=pltpu.SMEM),
                  pl.BlockSpec((tc, LANES), lambda i: (i, 0)),
                  pl.BlockSpec((tc, d), lambda i: (i, 0)),
                  pl.BlockSpec((1, 1, d), lambda i: ((i // per_seq) * N_ADA + gate_idx, 0, 0)),
                  pl.BlockSpec(memory_space=pl.ANY)],
        out_specs=pl.BlockSpec((tc, d), lambda i: (i, 0)),
        out_shape=jax.ShapeDtypeStruct((n, d), F32),
        scratch_shapes=[pltpu.VMEM((TOP_K, tc, d), F32), pltpu.SemaphoreType.DMA(())],
        compiler_params=_params(("arbitrary",), 32),
        name="moe_combine",
    )(pos_flat, meta, x, mod, yg)


def moe_ffn(h, router, w_gu, w_down, x, mod, gate_idx, seq):
    n, d = h.shape
    meta, counts = moe_route(h, router)
    cnt = counts[0, :N_EXPERTS].astype(I32)
    tiles_e = (cnt + MOE_TILE - 1) // MOE_TILE
    tile_end = jnp.cumsum(tiles_e)
    offset = (tile_end - tiles_e) * MOE_TILE
    n_tiles = (n * TOP_K) // MOE_TILE + N_EXPERTS
    tile_expert = jnp.minimum(
        jnp.sum(jnp.arange(n_tiles, dtype=I32)[:, None] >= tile_end[None, :], axis=1), N_EXPERTS - 1).astype(I32)
    n_used = tile_end[-1:].astype(I32)
    tile_expert = jnp.where(jnp.arange(n_tiles) < n_used[0], tile_expert, tile_expert[jnp.maximum(n_used[0] - 1, 0)])
    expert = meta[:, 0:TOP_K].astype(I32)
    rank = meta[:, 4:4 + TOP_K].astype(I32)
    pos_flat = (offset[expert] + rank).reshape(-1)
    hg = moe_scatter(h, pos_flat, n_tiles * MOE_TILE)
    yg = ffn_grouped(hg, w_gu, w_down, tile_expert, n_used)
    return moe_combine(yg, pos_flat, meta, x, mod, gate_idx, seq)


def _gelu_tanh(x):
    return 0.5 * x * (1.0 + jnp.tanh(0.7978845608028654 * (x + 0.044715 * x * x * x)))


def _nsa_prep_kernel(kc_ref, vc_ref, ks_ref, vs_ref, kw_ref, vw_ref, kg_ref, pos_ref, w1_ref, w2_ref,
                     kso_ref, vso_ref, kwo_ref, vwo_ref, kco_ref, vco_ref, *, nc):
    kso_ref[0, 0] = _rms(ks_ref[0], kg_ref[1:2, :]).astype(BF16)
    kwo_ref[0, 0] = _rms(kw_ref[0], kg_ref[2:3, :]).astype(BF16)
    vso_ref[0, 0] = vs_ref[0].astype(BF16)
    vwo_ref[0, 0] = vw_ref[0].astype(BF16)

    half = NSA_CMP_BLOCK // 2

    def compress(src_ref, j):
        first = jnp.zeros((nc, NSA_DH), F32)
        second = jnp.zeros((nc, NSA_DH), F32)
        for l in range(half):
            rows = src_ref[0, pl.ds(l, nc, stride=NSA_CMP_STRIDE), :]
            first += jnp.dot((rows + pos_ref[j, l:l + 1, :]).astype(BF16), w1_ref[j, l].astype(BF16),
                             preferred_element_type=F32)
            second += jnp.dot((rows + pos_ref[j, half + l:half + l + 1, :]).astype(BF16),
                              w1_ref[j, half + l].astype(BF16), preferred_element_type=F32)
        pre = first + pltpu.roll(second, nc - 1, axis=0)
        hid = _gelu_tanh(pre)
        return jnp.dot(hid.astype(BF16), w2_ref[j].astype(BF16), preferred_element_type=F32)

    kco_ref[0, 0] = _rms(compress(kc_ref, 0), kg_ref[0:1, :]).astype(BF16)
    vco_ref[0, 0] = compress(vc_ref, 1).astype(BF16)


def nsa_prep(proj, k_gain, cmp_pos, cmp_w1, cmp_w2):
    b, t, _ = proj.shape
    g, dh = NSA_GROUPS, NSA_DH
    nc = t // NSA_CMP_STRIDE
    qblocks = NSA_HEADS

    def col(j):
        return pl.BlockSpec((1, t, dh), lambda i, gg: (i, 0, qblocks + j * g + gg))

    full = lambda shape: pl.BlockSpec(shape, lambda i, gg: (0,) * len(shape))
    seq_out = pl.BlockSpec((1, 1, t, dh), lambda i, gg: (i, gg, 0, 0))
    cmp_out = pl.BlockSpec((1, 1, nc, dh), lambda i, gg: (i, gg, 0, 0))
    return pl.pallas_call(
        functools.partial(_nsa_prep_kernel, nc=nc),
        grid=(b, g),
        in_specs=[col(0), col(1), col(2), col(3), col(4), col(5),
                  full((3, dh)), full((2, NSA_CMP_BLOCK, dh)), full((2, NSA_CMP_BLOCK, dh, dh)), full((2, dh, dh))],
        out_specs=[seq_out, seq_out, seq_out, seq_out, cmp_out, cmp_out],
        out_shape=[jax.ShapeDtypeStruct((b, g, t, dh), BF16)] * 4 + [jax.ShapeDtypeStruct((b, g, nc, dh), BF16)] * 2,
        compiler_params=_params(("parallel", "parallel"), 48),
        name="nsa_prep",
    )(proj, proj, proj, proj, proj, proj, k_gain, cmp_pos, cmp_w1, cmp_w2)


def _online_softmax_step(carry, s_heads, ok, v, tq):
    m, l, acc = carry
    s = jnp.concatenate([jnp.where(ok, sh, NEG_BIG) for sh in s_heads], axis=0)
    m_new = jnp.maximum(m, jnp.max(s, axis=-1, keepdims=True))
    alpha = jnp.exp(m - m_new)
    p = jnp.exp(s - m_new)
    l = alpha * l + jnp.sum(p, axis=-1, keepdims=True)
    acc = alpha * acc + jnp.dot(p.astype(BF16), v, preferred_element_type=F32)
    return m_new, l, acc


def _nsa_attn_kernel(slope_ref, q_ref, gl_ref, ks_ref, vs_ref, kw_ref, vw_ref, kc_ref, vc_ref, qg_ref,
                     o_ref, selx_ref, *, tq, ck, t_len, nc, nbl):
    grp = pl.program_id(1)
    t0 = pl.program_id(2) * tq
    rep, dh = NSA_REP, NSA_DH
    scale = dh ** -0.5
    slopes = [slope_ref[grp * rep + r] for r in range(rep)]

    q = q_ref[0]
    qn = jnp.concatenate([_rms(q[:, r * dh:(r + 1) * dh], qg_ref[...]) for r in range(rep)], axis=0)
    qb = (qn * scale).astype(BF16)
    tpos = t0 + lax.broadcasted_iota(I32, (tq, 1), 0)

    cmp_end = lax.broadcasted_iota(I32, (1, nc), 1) * NSA_CMP_STRIDE + (NSA_CMP_BLOCK - 1)
    dist_c = (tpos - cmp_end).astype(F32)
    ok_c = dist_c >= 0
    s_c = lax.dot_general(qb, kc_ref[0, 0], _NT, preferred_element_type=F32)
    p_heads = []
    for r in range(rep):
        s = jnp.where(ok_c, s_c[r * tq:(r + 1) * tq] - slopes[r] * dist_c, NEG_BIG)
        m = jnp.max(s, axis=-1, keepdims=True)
        e = jnp.where(ok_c, jnp.exp(s - m), 0.0)
        p_heads.append(e / jnp.maximum(jnp.sum(e, axis=-1, keepdims=True), 1e-30))
    o_cmp = jnp.dot(jnp.concatenate(p_heads, axis=0).astype(BF16), vc_ref[0, 0], preferred_element_type=F32)

    p_sum = p_heads[0]
    for r in range(1, rep):
        p_sum = p_sum + p_heads[r]
    cstart = lax.broadcasted_iota(I32, (nc, nbl), 0) * NSA_CMP_STRIDE
    bstart = lax.broadcasted_iota(I32, (nc, nbl), 1) * NSA_SEL_BLOCK
    overlap = jnp.where((cstart < bstart + NSA_SEL_BLOCK) & (cstart + NSA_CMP_BLOCK > bstart), 1.0, 0.0).astype(BF16)
    imp = jnp.zeros((tq, nbl), F32)
    for part in _split3(p_sum):
        imp += jnp.dot(part, overlap, preferred_element_type=F32)
    blk = lax.broadcasted_iota(I32, (tq, nbl), 1)
    cur = tpos >> 6
    forced = (blk == 0) | (blk == cur) | (blk == cur - 1)
    future = blk * NSA_SEL_BLOCK > tpos
    score = jnp.where(future, -SEL_FORCE, jnp.where(forced, SEL_FORCE, imp))
    blk_f = blk.astype(F32)
    sel = jnp.zeros((tq, nbl), F32)
    taken = jnp.float32(-3e38)
    for _ in range(min(NSA_TOPK, t_len // NSA_SEL_BLOCK)):
        best = jnp.max(score, axis=-1, keepdims=True)
        first = jnp.min(jnp.where(score == best, blk_f, float(nbl)), axis=-1, keepdims=True)
        pick = blk_f == first
        sel = jnp.where(pick, 1.0, sel)
        score = jnp.where(pick, taken, score)
    eblk = lax.broadcasted_iota(I32, (nbl, t_len), 0)
    ekey = lax.broadcasted_iota(I32, (nbl, t_len), 1) >> 6
    expand = jnp.where(eblk == ekey, 1.0, 0.0).astype(BF16)
    selx_ref[...] = jnp.dot(sel.astype(BF16), expand, preferred_element_type=F32)

    rows = rep * tq
    init = (jnp.full((rows, 1), NEG_BIG, F32), jnp.zeros((rows, 1), F32), jnp.zeros((rows, dh), F32))

    def sel_chunk(c, carry):
        k0 = pl.multiple_of(c * ck, ck)
        k = ks_ref[0, 0, pl.ds(k0, ck), :]
        v = vs_ref[0, 0, pl.ds(k0, ck), :]
        dist = (tpos - (k0 + lax.broadcasted_iota(I32, (1, ck), 1))).astype(F32)
        ok = (selx_ref[:, pl.ds(k0, ck)] > 0.5) & (dist >= 0)
        s_all = lax.dot_general(qb, k, _NT, preferred_element_type=F32)
        s_heads = [s_all[r * tq:(r + 1) * tq] - slopes[r] * dist for r in range(rep)]
        return _online_softmax_step(carry, s_heads, ok, v, tq)

    n_chunks = (t0 + tq + ck - 1) // ck
    _, l_s, acc_s = lax.fori_loop(0, n_chunks, sel_chunk, init)
    o_slc = acc_s / jnp.maximum(l_s, 1e-30)

    carry = init
    for w in range(NSA_WINDOW // tq, -1, -1):
        kpos0 = t0 - w * tq
        k0 = pl.multiple_of(jnp.maximum(kpos0, 0), tq)
        k = kw_ref[0, 0, pl.ds(k0, tq), :]
        v = vw_ref[0, 0, pl.ds(k0, tq), :]
        kpos = kpos0 + lax.broadcasted_iota(I32, (1, tq), 1)
        dist_i = tpos - kpos
        dist = dist_i.astype(F32)
        ok = (dist_i >= 0) & (dist_i < NSA_WINDOW) & (kpos >= 0)
        s_all = lax.dot_general(qb, k, _NT, preferred_element_type=F32)
        s_heads = [s_all[r * tq:(r + 1) * tq] - slopes[r] * dist for r in range(rep)]
        carry = _online_softmax_step(carry, s_heads, ok, v, tq)
    o_win = carry[2] / jnp.maximum(carry[1], 1e-30)

    gates = jax.nn.sigmoid(gl_ref[0])
    lane = lax.broadcasted_iota(I32, (tq, LANES), 1)

    def gate(r, j):
        return jnp.sum(jnp.where(lane == 3 * r + j, gates, 0.0), axis=-1, keepdims=True)

    for r in range(rep):
        sl = slice(r * tq, (r + 1) * tq)
        o = gate(r, 0) * o_cmp[sl] + gate(r, 1) * o_slc[sl] + gate(r, 2) * o_win[sl]
        o_ref[0, :, r * dh:(r + 1) * dh] = o.astype(o_ref.dtype)


def nsa_attention(proj, ksn, vs, kwn, vw, kcmp, vcmp, q_gain, slopes):
    b, t, _ = proj.shape
    g, rep, dh = NSA_GROUPS, NSA_REP, NSA_DH
    tq = 128
    ck = min(512, t)
    nc = t // NSA_CMP_STRIDE
    nbl = max(LANES, t // NSA_SEL_BLOCK)
    gate_block = (NSA_HEADS + 6 * g)
    seq_in = pl.BlockSpec((1, 1, t, dh), lambda i, gg, qi, sl: (i, gg, 0, 0))
    cmp_in = pl.BlockSpec((1, 1, nc, dh), lambda i, gg, qi, sl: (i, gg, 0, 0))
    grid_spec = pltpu.PrefetchScalarGridSpec(
        num_scalar_prefetch=1,
        grid=(b, g, t // tq),
        in_specs=[pl.BlockSpec((1, tq, rep * dh), lambda i, gg, qi, sl: (i, qi, gg)),
                  pl.BlockSpec((1, tq, LANES), lambda i, gg, qi, sl: (i, qi, gate_block + gg)),
                  seq_in, seq_in, seq_in, seq_in, cmp_in, cmp_in,
                  pl.BlockSpec((1, dh), lambda i, gg, qi, sl: (0, 0))],
        out_specs=pl.BlockSpec((1, tq, rep * dh), lambda i, gg, qi, sl: (i, qi, gg)),
        scratch_shapes=[pltpu.VMEM((tq, t), F32)],
    )
    return pl.pallas_call(
        functools.partial(_nsa_attn_kernel, tq=tq, ck=ck, t_len=t, nc=nc, nbl=nbl),
        grid_spec=grid_spec,
        out_shape=jax.ShapeDtypeStruct((b, t, g * rep * dh), BF16),
        compiler_params=_params(("parallel", "parallel", "arbitrary"), 48),
        name="nsa_attn",
    )(slopes, proj, proj, ksn, vs, kwn, vw, kcmp, vcmp, q_gain.reshape(1, dh))


def nsa_mixer(h, w_in, w_out, q_gain, k_gain, cmp_pos, cmp_w1, cmp_w2, x, mod, batch, seq):
    d = h.shape[1]
    g, rep, dh = NSA_GROUPS, NSA_REP, NSA_DH
    n_main = NSA_HEADS * dh + 6 * g * dh
    w_gate = w_in[:, n_main:].reshape(d, g, rep * 3)
    w_gate = jnp.pad(w_gate, ((0, 0), (0, 0), (0, LANES - rep * 3))).reshape(d, g * LANES)
    w_all = jnp.concatenate([w_in[:, :n_main], w_gate], axis=1).astype(BF16)
    proj = matmul(h, w_all, F32).reshape(batch, seq, -1)
    ksn, vs, kwn, vw, kcmp, vcmp = nsa_prep(proj, k_gain, cmp_pos, cmp_w1, cmp_w2)
    slopes = jnp.exp2(-8.0 * jnp.arange(1, NSA_HEADS + 1, dtype=F32) / NSA_HEADS)
    o = nsa_attention(proj, ksn, vs, kwn, vw, kcmp, vcmp, q_gain, slopes)
    return matmul_residual(o.reshape(batch * seq, -1), w_out.astype(BF16), x, mod, 2, seq)


def _hgrn_kernel(q_ref, f_ref, v_ref, g_ref, lb_ref, og_ref, o_ref, *, t_len, chunk):
    blk = HGRN_BLOCK
    dk = q_ref.shape[-1]
    lb = lb_ref[0]
    log_lb = jnp.log(lb)
    log_1m_lb = jnp.log1p(-lb)
    row_in_blk = lax.broadcasted_iota(I32, (chunk, dk), 0) & (blk - 1)

    def shift_rows(x, s):
        return pltpu.roll(x, s, axis=0)

    def chunk_step(c, state_t):
        r0 = pl.multiple_of(c * chunk, chunk)
        q = _silu(q_ref[0, pl.ds(r0, chunk), :])
        z = f_ref[0, pl.ds(r0, chunk), :]
        v = v_ref[0, pl.ds(r0, chunk), :]
        log_sig = jnp.minimum(z, 0.0) - jnp.log1p(jnp.exp(-jnp.abs(z)))
        a2 = log_1m_lb + log_sig
        hi = jnp.maximum(log_lb, a2)
        logf = hi + jnp.log1p(jnp.exp(-jnp.abs(log_lb - a2)))
        k = (1.0 - lb) * jax.nn.sigmoid(-z)
        bsum = logf
        s = 1
        while s < blk:
            bsum = bsum + jnp.where(row_in_blk >= s, shift_rows(bsum, s), 0.0)
            s *= 2
        o = jnp.zeros((chunk, dk), F32)
        for dlt in range(blk):
            ok = row_in_blk >= dlt
            kd = shift_rows(k, dlt) if dlt else k
            bd = shift_rows(bsum, dlt) if dlt else bsum
            vd = shift_rows(v, dlt) if dlt else v
            w = q * kd * jnp.exp(jnp.where(ok, bsum - bd, 0.0))
            a = jnp.sum(jnp.where(ok, w, 0.0), axis=-1, keepdims=True)
            o = o + a * vd
        outs = []
        for i in range(chunk // blk):
            sl = slice(i * blk, (i + 1) * blk)
            b_i = bsum[sl]
            b_last = b_i[blk - 1:blk, :]
            q_hat = (q[sl] * jnp.exp(b_i)).astype(BF16)
            outs.append(o[sl] + lax.dot_general(q_hat, state_t.astype(BF16), _NT, preferred_element_type=F32))
            k_hat = (k[sl] * jnp.exp(b_last - b_i)).astype(BF16)
            state_t = jnp.exp(b_last) * state_t + lax.dot_general(
                v[sl].astype(BF16), k_hat, _TN, preferred_element_type=F32)
        o_all = jnp.concatenate(outs, axis=0)
        gate = _silu(g_ref[0, pl.ds(r0, chunk), :])
        o_ref[0, pl.ds(r0, chunk), :] = (_rms(o_all, og_ref[...]) * gate).astype(o_ref.dtype)
        return state_t

    lax.fori_loop(0, t_len // chunk, chunk_step, jnp.zeros((dk, dk), F32))


def hgrn_recurrence(proj, lb, o_gain):
    b, t, _ = proj.shape
    hh = HGRN_HEADS
    dk = lb.shape[0] // hh
    chunk = 64

    def col(j):
        return pl.BlockSpec((1, t, dk), lambda i, h: (i, 0, j * hh + h))

    return pl.pallas_call(
        functools.partial(_hgrn_kernel, t_len=t, chunk=chunk),
        grid=(b, hh),
        in_specs=[col(0), col(1), col(2), col(3),
                  pl.BlockSpec((1, 1, dk), lambda i, h: (h, 0, 0)),
                  pl.BlockSpec((1, dk), lambda i, h: (0, 0))],
        out_specs=pl.BlockSpec((1, t, dk), lambda i, h: (i, 0, h)),
        out_shape=jax.ShapeDtypeStruct((b, t, hh * dk), BF16),
        compiler_params=_params(("parallel", "parallel"), 48),
        name="hgrn",
    )(proj, proj, proj, proj, lb.reshape(hh, 1, dk), o_gain.reshape(1, dk))


def hgrn2_mixer(h, w_in, w_out, o_gain, lb, x, mod, batch, seq):
    proj = matmul(h, w_in.astype(BF16), F32).reshape(batch, seq, -1)
    o = hgrn_recurrence(proj, lb, o_gain)
    return matmul_residual(o.reshape(batch * seq, -1), w_out.astype(BF16), x, mod, 2, seq)


def _swa_kernel(sink_ref, q_ref, kp_ref, kc_ref, vp_ref, vc_ref, qg_ref, kg_ref, o_ref, *, tq):
    t0 = pl.program_id(1) * tq
    dh, rep = SWA_DH, SWA_REP
    scale = dh ** -0.5
    q = q_ref[0]
    kcat = jnp.concatenate([kp_ref[0], kc_ref[0]], axis=0)
    vcat = jnp.concatenate([vp_ref[0], vc_ref[0]], axis=0)
    tpos = t0 + lax.broadcasted_iota(I32, (tq, 1), 0)
    kpos = t0 - tq + lax.broadcasted_iota(I32, (1, 2 * tq), 1)
    dist_i = tpos - kpos
    dist = dist_i.astype(F32)
    ok = (dist_i >= 0) & (dist_i < SWA_WINDOW) & (kpos >= 0)
    outs = []
    for g in range(SWA_GROUPS):
        k = _rms(kcat[:, g * dh:(g + 1) * dh], kg_ref[...]).astype(BF16)
        v = vcat[:, g * dh:(g + 1) * dh].astype(BF16)
        heads = [g * rep + r for r in range(rep)]
        qs = jnp.concatenate([_rms(q[:, hd * dh:(hd + 1) * dh], qg_ref[...]) for hd in heads], axis=0)
        s_all = lax.dot_general((qs * scale).astype(BF16), k, _NT, preferred_element_type=F32)
        ps = []
        for r, hd in enumerate(heads):
            slope = 2.0 ** (-8.0 * (hd + 1) / SWA_HEADS)
            sink = sink_ref[hd]
            s = jnp.where(ok, s_all[r * tq:(r + 1) * tq] - slope * dist, NEG_BIG)
            m = jnp.maximum(jnp.max(s, axis=-1, keepdims=True), sink)
            e = jnp.where(ok, jnp.exp(s - m), 0.0)
            den = jnp.sum(e, axis=-1, keepdims=True) + jnp.exp(sink - m)
            ps.append(e / jnp.maximum(den, 1e-30))
        o = jnp.dot(jnp.concatenate(ps, axis=0).astype(BF16), v, preferred_element_type=F32)
        outs += [o[r * tq:(r + 1) * tq] for r in range(rep)]
    o_ref[0] = jnp.concatenate(outs, axis=-1).astype(o_ref.dtype)


def swa_attention(proj, q_gain, k_gain, sinks):
    b, t, _ = proj.shape
    hd = SWA_HEADS * SWA_DH
    kv = SWA_GROUPS * SWA_DH
    tq = SWA_WINDOW
    kblk = hd // kv
    prev = lambda qi: jnp.maximum(qi - 1, 0)
    grid_spec = pltpu.PrefetchScalarGridSpec(
        num_scalar_prefetch=1,
        grid=(b, t // tq),
        in_specs=[pl.BlockSpec((1, tq, hd), lambda i, qi, sk: (i, qi, 0)),
                  pl.BlockSpec((1, tq, kv), lambda i, qi, sk: (i, prev(qi), kblk)),
                  pl.BlockSpec((1, tq, kv), lambda i, qi, sk: (i, qi, kblk)),
                  pl.BlockSpec((1, tq, kv), lambda i, qi, sk: (i, prev(qi), kblk + 1)),
                  pl.BlockSpec((1, tq, kv), lambda i, qi, sk: (i, qi, kblk + 1)),
                  pl.BlockSpec((1, SWA_DH), lambda i, qi, sk: (0, 0)),
                  pl.BlockSpec((1, SWA_DH), lambda i, qi, sk: (0, 0))],
        out_specs=pl.BlockSpec((1, tq, hd), lambda i, qi, sk: (i, qi, 0)),
    )
    return pl.pallas_call(
        functools.partial(_swa_kernel, tq=tq),
        grid_spec=grid_spec,
        out_shape=jax.ShapeDtypeStruct((b, t, hd), BF16),
        compiler_params=_params(("parallel", "parallel"), 40),
        name="swa",
    )(sinks, proj, proj, proj, proj, proj, q_gain.reshape(1, -1), k_gain.reshape(1, -1))


def swa_mixer(h, w_in, w_out, q_gain, k_gain, sinks, x, mod, batch, seq):
    proj = matmul(h, w_in.astype(BF16), F32).reshape(batch, seq, -1)
    o = swa_attention(proj, q_gain, k_gain, sinks)
    return matmul_residual(o.reshape(batch * seq, -1), w_out.astype(BF16), x, mod, 2, seq)


def kernel(x, c, norm_mix_g, norm_ffn_g, ada_w, ada_b, nsa_w_in, nsa_w_out, nsa_q_gain, nsa_k_gain,
           nsa_cmp_pos, nsa_cmp_w1, nsa_cmp_w2, hgrn_w_in, hgrn_w_out, hgrn_o_gain, hgrn_lb_logits,
           swa_w_in, swa_w_out, swa_q_gain, swa_k_gain, swa_sinks, ffn_w_gu, ffn_w_down,
           moe_router, moe_w_gu, moe_w_down):
    batch, seq, d = x.shape
    depth = ada_w.shape[0]
    n = batch * seq
    mod_all = ada_all(c, ada_w, ada_b)
    lb_all = jax.nn.softmax(hgrn_lb_logits.astype(F32), axis=0)
    n_a = n_b = n_c = n_d = n_m = 0
    x2 = x.reshape(n, d)
    for i in range(depth):
        mod = mod_all[i].reshape(batch * N_ADA, 1, d)
        h = norm_mod(x2.reshape(batch, seq, d), norm_mix_g[i], mod, 1, 0, BF16).reshape(n, d)
        kind = i % 3
        if kind == 0:
            x2 = nsa_mixer(h, nsa_w_in[n_a], nsa_w_out[n_a], nsa_q_gain[n_a], nsa_k_gain[n_a],
                           nsa_cmp_pos[n_a], nsa_cmp_w1[n_a], nsa_cmp_w2[n_a], x2, mod, batch, seq)
            n_a += 1
        elif kind == 1:
            lb = jnp.sum(lb_all[1:i + 1], axis=0)
            x2 = hgrn2_mixer(h, hgrn_w_in[n_b], hgrn_w_out[n_b], hgrn_o_gain[n_b], lb, x2, mod, batch, seq)
            n_b += 1
        else:
            x2 = swa_mixer(h, swa_w_in[n_c], swa_w_out[n_c], swa_q_gain[n_c], swa_k_gain[n_c], swa_sinks[n_c],
                           x2, mod, batch, seq)
            n_c += 1
        if i % 2 == 0:
            h = norm_mod(x2.reshape(batch, seq, d), norm_ffn_g[i], mod, 4, 3, BF16).reshape(n, d)
            x2 = ffn_dense(h, ffn_w_gu[n_d].astype(BF16), ffn_w_down[n_d].astype(BF16), x2, mod, 5, seq)
            n_d += 1
        else:
            h = norm_mod(x2.reshape(batch, seq, d), norm_ffn_g[i], mod, 4, 3, F32).reshape(n, d)
            x2 = moe_ffn(h, moe_router[n_m], moe_w_gu[n_m].astype(BF16), moe_w_down[n_m].astype(BF16),
                         x2, mod, 5, seq)
            n_m += 1
    return x2.reshape(batch, seq, d)
```

```python
import functools

import jax
import jax.numpy as jnp
from jax import lax
from jax.experimental import pallas as pl
from jax.experimental.pallas import tpu as pltpu

F32 = jnp.float32
BF16 = jnp.bfloat16
I32 = jnp.int32

RMS_EPS = 1e-6
NEG_BIG = -1e30
SEL_FORCE = 1e9
N_ADA = 6

LANES = 128
SUBLANES = 8
VMEM_BYTES_V7X = 64 * 1024 * 1024

NSA_HEADS, NSA_GROUPS, NSA_DH = 16, 4, 128
NSA_REP = NSA_HEADS // NSA_GROUPS
NSA_CMP_BLOCK, NSA_CMP_STRIDE, NSA_SEL_BLOCK, NSA_TOPK, NSA_WINDOW = 32, 16, 64, 16, 512
HGRN_HEADS, HGRN_BLOCK = 16, 16
SWA_HEADS, SWA_GROUPS, SWA_DH, SWA_WINDOW = 32, 4, 64, 128
SWA_REP = SWA_HEADS // SWA_GROUPS
N_EXPERTS, TOP_K = 8, 2

_NT = (((1,), (1,)), ((), ()))
_TN = (((0,), (0,)), ((), ()))


def _params(sem, vmem_mb):
    assert vmem_mb * 1024 * 1024 < VMEM_BYTES_V7X
    return pltpu.CompilerParams(dimension_semantics=sem, vmem_limit_bytes=vmem_mb * 1024 * 1024)


def _rms(x, gain):
    return x * lax.rsqrt(jnp.mean(x * x, axis=-1, keepdims=True) + RMS_EPS) * gain


def _silu(x):
    return x * jax.nn.sigmoid(x)


def _ada_kernel(c_ref, w_ref, b_ref, o_ref):
    cond = _silu(c_ref[...])
    o_ref[0] = jnp.dot(cond, w_ref[0], preferred_element_type=F32,
                       precision=lax.Precision.HIGHEST) + b_ref[0]


def ada_all(c, ada_w, ada_b):
    depth, d, n = ada_w.shape
    b = c.shape[0]
    tn = 1024
    return pl.pallas_call(
        _ada_kernel,
        grid=(depth, n // tn),
        in_specs=[pl.BlockSpec((b, d), lambda l, j: (0, 0)),
                  pl.BlockSpec((1, d, tn), lambda l, j: (l, 0, j)),
                  pl.BlockSpec((1, 1, tn), lambda l, j: (l, 0, j))],
        out_specs=pl.BlockSpec((1, b, tn), lambda l, j: (l, 0, j)),
        out_shape=jax.ShapeDtypeStruct((depth, b, n), F32),
        compiler_params=_params(("arbitrary", "arbitrary"), 40),
        name="ada",
    )(c, ada_w, ada_b.reshape(depth, 1, n))


def _norm_mod_kernel(x_ref, g_ref, sc_ref, sh_ref, o_ref):
    y = _rms(x_ref[0], g_ref[...])
    o_ref[0] = (y * (1.0 + sc_ref[0]) + sh_ref[0]).astype(o_ref.dtype)


def norm_mod(x, gain, mod, sc_idx, sh_idx, out_dtype):
    b, t, d = x.shape
    tm = 512
    return pl.pallas_call(
        _norm_mod_kernel,
        grid=(b, t // tm),
        in_specs=[pl.BlockSpec((1, tm, d), lambda i, j: (i, j, 0)),
                  pl.BlockSpec((1, d), lambda i, j: (0, 0)),
                  pl.BlockSpec((1, 1, d), lambda i, j: (i * N_ADA + sc_idx, 0, 0)),
                  pl.BlockSpec((1, 1, d), lambda i, j: (i * N_ADA + sh_idx, 0, 0))],
        out_specs=pl.BlockSpec((1, tm, d), lambda i, j: (i, j, 0)),
        out_shape=jax.ShapeDtypeStruct((b, t, d), out_dtype),
        compiler_params=_params(("parallel", "parallel"), 40),
        name="norm_mod",
    )(x, gain.reshape(1, d), mod, mod)


def _mm_kernel(a_ref, w_ref, o_ref):
    o_ref[...] = jnp.dot(a_ref[...], w_ref[...], preferred_element_type=F32).astype(o_ref.dtype)


def _mm_res_kernel(a_ref, w_ref, x_ref, g_ref, o_ref):
    y = jnp.dot(a_ref[...], w_ref[...], preferred_element_type=F32)
    o_ref[...] = x_ref[...] + g_ref[0] * y


def matmul(a, w, out_dtype, tn=512):
    m, k = a.shape
    n = w.shape[1]
    tm = min(1024, m)
    return pl.pallas_call(
        _mm_kernel,
        grid=(m // tm, n // tn),
        in_specs=[pl.BlockSpec((tm, k), lambda i, j: (i, 0)),
                  pl.BlockSpec((k, tn), lambda i, j: (0, j))],
        out_specs=pl.BlockSpec((tm, tn), lambda i, j: (i, j)),
        out_shape=jax.ShapeDtypeStruct((m, n), out_dtype),
        compiler_params=_params(("parallel", "arbitrary"), 40),
        name="matmul",
    )(a, w)


def matmul_residual(a, w, x, mod, gate_idx, seq):
    m, k = a.shape
    n = w.shape[1]
    tm = min(1024, seq)
    tn = 512
    per_seq = seq // tm
    return pl.pallas_call(
        _mm_res_kernel,
        grid=(m // tm, n // tn),
        in_specs=[pl.BlockSpec((tm, k), lambda i, j: (i, 0)),
                  pl.BlockSpec((k, tn), lambda i, j: (0, j)),
                  pl.BlockSpec((tm, tn), lambda i, j: (i, j)),
                  pl.BlockSpec((1, 1, tn), lambda i, j: ((i // per_seq) * N_ADA + gate_idx, 0, j))],
        out_specs=pl.BlockSpec((tm, tn), lambda i, j: (i, j)),
        out_shape=jax.ShapeDtypeStruct((m, n), F32),
        compiler_params=_params(("parallel", "arbitrary"), 40),
        name="matmul_residual",
    )(a, w, x, mod)


def _ffn_step(h_bf16, wg_ref, wu_ref, wd_ref):
    g = jnp.dot(h_bf16, wg_ref[...], preferred_element_type=F32)
    u = jnp.dot(h_bf16, wu_ref[...], preferred_element_type=F32)
    a = (_silu(g) * u).astype(BF16)
    return jnp.dot(a, wd_ref[...], preferred_element_type=F32)


def _ffn_dense_kernel(h_ref, wg_ref, wu_ref, wd_ref, x_ref, g_ref, o_ref, acc_ref, *, nf):
    f = pl.program_id(1)
    part = _ffn_step(h_ref[...], wg_ref, wu_ref, wd_ref)

    @pl.when(f == 0)
    def _():
        acc_ref[...] = part

    @pl.when(f > 0)
    def _():
        acc_ref[...] += part

    @pl.when(f == nf - 1)
    def _():
        o_ref[...] = x_ref[...] + g_ref[0] * acc_ref[...]


def ffn_dense(h, w_gu, w_down, x, mod, gate_idx, seq):
    m, d = h.shape
    dff = w_down.shape[0]
    tm, tf = 512, 512
    nf = dff // tf
    per_seq = seq // tm
    return pl.pallas_call(
        functools.partial(_ffn_dense_kernel, nf=nf),
        grid=(m // tm, nf),
        in_specs=[pl.BlockSpec((tm, d), lambda i, f: (i, 0)),
                  pl.BlockSpec((d, tf), lambda i, f: (0, f)),
                  pl.BlockSpec((d, tf), lambda i, f: (0, nf + f)),
                  pl.BlockSpec((tf, d), lambda i, f: (f, 0)),
                  pl.BlockSpec((tm, d), lambda i, f: (i, 0)),
                  pl.BlockSpec((1, 1, d), lambda i, f: ((i // per_seq) * N_ADA + gate_idx, 0, 0))],
        out_specs=pl.BlockSpec((tm, d), lambda i, f: (i, 0)),
        out_shape=jax.ShapeDtypeStruct((m, d), F32),
        scratch_shapes=[pltpu.VMEM((tm, d), F32)],
        compiler_params=_params(("parallel", "arbitrary"), 48),
        name="ffn_dense",
    )(h, w_gu, w_gu, w_down, x, mod)


def _ffn_grouped_kernel(te_ref, nu_ref, h_ref, wg_ref, wu_ref, wd_ref, o_ref, acc_ref, hb_ref, *, nf):
    i = pl.program_id(0)
    f = pl.program_id(1)
    used = i < nu_ref[0]

    @pl.when(used & (f == 0))
    def _():
        hb_ref[...] = h_ref[...].astype(BF16)

    @pl.when(used)
    def _():
        part = _ffn_step(hb_ref[...], wg_ref, wu_ref, wd_ref)

        @pl.when(f == 0)
        def _():
            acc_ref[...] = part

        @pl.when(f > 0)
        def _():
            acc_ref[...] += part

    @pl.when(f == nf - 1)
    def _():
        o_ref[...] = jnp.where(used, acc_ref[...], 0.0)


MOE_TILE = 512


def ffn_grouped(hg, w_gu, w_down, tile_expert, n_used):
    mp, d = hg.shape
    dff = w_down.shape[1]
    tm, tf = MOE_TILE, 512
    nf = dff // tf

    def fsel(i, f, nu):
        return jnp.where(i < nu[0], f, 0)

    grid_spec = pltpu.PrefetchScalarGridSpec(
        num_scalar_prefetch=2,
        grid=(mp // tm, nf),
        in_specs=[pl.BlockSpec((tm, d), lambda i, f, te, nu: (i, 0)),
                  pl.BlockSpec((None, d, tf), lambda i, f, te, nu: (te[i], 0, fsel(i, f, nu))),
                  pl.BlockSpec((None, d, tf), lambda i, f, te, nu: (te[i], 0, nf + fsel(i, f, nu))),
                  pl.BlockSpec((None, tf, d), lambda i, f, te, nu: (te[i], fsel(i, f, nu), 0))],
        out_specs=pl.BlockSpec((tm, d), lambda i, f, te, nu: (i, 0)),
        scratch_shapes=[pltpu.VMEM((tm, d), F32), pltpu.VMEM((tm, d), BF16)],
    )
    return pl.pallas_call(
        functools.partial(_ffn_grouped_kernel, nf=nf),
        grid_spec=grid_spec,
        out_shape=jax.ShapeDtypeStruct((mp, d), F32),
        compiler_params=_params(("arbitrary", "arbitrary"), 48),
        name="ffn_grouped",
    )(tile_expert, n_used, hg, w_gu, w_gu, w_down)


def _split3(x):
    hi = x.astype(BF16)
    r1 = x - hi.astype(F32)
    mid = r1.astype(BF16)
    lo = (r1 - mid.astype(F32)).astype(BF16)
    return hi, mid, lo


def _route_kernel(h_ref, r_ref, meta_ref, cnt_ref, carry_ref, *, tr):
    i = pl.program_id(0)

    @pl.when(i == 0)
    def _():
        carry_ref[...] = jnp.zeros_like(carry_ref)

    logits = jnp.dot(h_ref[...], r_ref[...], preferred_element_type=F32,
                     precision=lax.Precision.HIGHEST)
    lane = lax.broadcasted_iota(I32, (tr, LANES), 1).astype(F32)
    logits = jnp.where(lane < N_EXPERTS, logits, -jnp.inf)
    m1 = jnp.max(logits, axis=-1, keepdims=True)
    e1 = jnp.min(jnp.where(logits == m1, lane, float(LANES)), axis=-1, keepdims=True)
    oh1 = lane == e1
    rest = jnp.where(oh1, -jnp.inf, logits)
    m2 = jnp.max(rest, axis=-1, keepdims=True)
    e2 = jnp.min(jnp.where(rest == m2, lane, float(LANES)), axis=-1, keepdims=True)
    oh2 = lane == e2
    ex = jnp.exp(m2 - m1)
    w1 = 1.0 / (1.0 + ex)
    w2 = ex / (1.0 + ex)
    oh = jnp.where(oh1 | oh2, 1.0, 0.0)
    row = lax.broadcasted_iota(I32, (tr, tr), 0)
    col = lax.broadcasted_iota(I32, (tr, tr), 1)
    strict_lower = jnp.where(col < row, 1.0, 0.0).astype(BF16)
    prefix = jnp.dot(strict_lower, oh.astype(BF16), preferred_element_type=F32) + carry_ref[...]
    r1 = jnp.sum(jnp.where(oh1, prefix, 0.0), axis=-1, keepdims=True)
    r2 = jnp.sum(jnp.where(oh2, prefix, 0.0), axis=-1, keepdims=True)
    total = carry_ref[...] + jnp.sum(oh, axis=0, keepdims=True)
    carry_ref[...] = total
    cnt_ref[...] = total
    meta = jnp.where(lane == 0, e1, jnp.where(lane == 1, e2, jnp.where(lane == 2, w1, jnp.where(
        lane == 3, w2, jnp.where(lane == 4, r1, jnp.where(lane == 5, r2, 0.0))))))
    meta_ref[...] = meta


def moe_route(h, router):
    n, d = h.shape
    tr = min(1024, n)
    rp = jnp.zeros((d, LANES), F32).at[:, :N_EXPERTS].set(router)
    return pl.pallas_call(
        functools.partial(_route_kernel, tr=tr),
        grid=(n // tr,),
        in_specs=[pl.BlockSpec((tr, d), lambda i: (i, 0)),
                  pl.BlockSpec((d, LANES), lambda i: (0, 0))],
        out_specs=[pl.BlockSpec((tr, LANES), lambda i: (i, 0)),
                   pl.BlockSpec((1, LANES), lambda i: (0, 0))],
        out_shape=[jax.ShapeDtypeStruct((n, LANES), F32), jax.ShapeDtypeStruct((1, LANES), F32)],
        scratch_shapes=[pltpu.VMEM((1, LANES), F32)],
        compiler_params=_params(("arbitrary",), 48),
        name="moe_route",
    )(h, rp)


def _row_copy(src_hbm, src_row, dst, dst_row, sem):
    return pltpu.make_async_copy(src_hbm.at[pl.ds(src_row, 1), :], dst.at[pl.ds(dst_row, 1), :], sem)


def _scatter_kernel(pos_ref, h_hbm, init_hbm, o_hbm, sem, *, ts):
    del init_hbm
    base = pl.program_id(0) * ts

    def start(t, c):
        for k in range(TOP_K):
            _row_copy(h_hbm, base + t, o_hbm, pos_ref[TOP_K * t + k], sem).start()
        return c

    def wait(t, c):
        for k in range(TOP_K):
            _row_copy(h_hbm, base + t, o_hbm, pos_ref[TOP_K * t + k], sem).wait()
        return c

    lax.fori_loop(0, ts, start, 0)
    lax.fori_loop(0, ts, wait, 0)


def moe_scatter(h, pos_flat, mp):
    n, d = h.shape
    ts = min(1024, n)
    return pl.pallas_call(
        functools.partial(_scatter_kernel, ts=ts),
        grid=(n // ts,),
        in_specs=[pl.BlockSpec((TOP_K * ts,), lambda i: (i,), memory_space=pltpu.SMEM),
                  pl.BlockSpec(memory_space=pl.ANY),
                  pl.BlockSpec(memory_space=pl.ANY)],
        out_specs=pl.BlockSpec(memory_space=pl.ANY),
        out_shape=jax.ShapeDtypeStruct((mp, d), F32),
        scratch_shapes=[pltpu.SemaphoreType.DMA(())],
        input_output_aliases={2: 0},
        compiler_params=_params(("arbitrary",), 32),
        name="moe_scatter",
    )(pos_flat, h, jnp.zeros((mp, d), F32))


def _gather_rows_kernel(tok_ref, h_hbm, o_ref, rows_ref, sem, *, tg):
    def start(t, c):
        _row_copy(h_hbm, tok_ref[t], rows_ref, t, sem).start()
        return c

    def wait(t, c):
        _row_copy(h_hbm, tok_ref[t], rows_ref, t, sem).wait()
        return c

    lax.fori_loop(0, tg, start, 0)
    lax.fori_loop(0, tg, wait, 0)
    o_ref[...] = rows_ref[...]


def moe_gather(h, slot_token):
    n, d = h.shape
    mp = slot_token.shape[0]
    tg = MOE_TILE
    return pl.pallas_call(
        functools.partial(_gather_rows_kernel, tg=tg),
        grid=(mp // tg,),
        in_specs=[pl.BlockSpec((tg,), lambda i: (i,), memory_space=pltpu.SMEM),
                  pl.BlockSpec(memory_space=pl.ANY)],
        out_specs=pl.BlockSpec((tg, d), lambda i: (i, 0)),
        out_shape=jax.ShapeDtypeStruct((mp, d), F32),
        scratch_shapes=[pltpu.VMEM((tg, d), F32), pltpu.SemaphoreType.DMA(())],
        compiler_params=_params(("arbitrary",), 32),
        name="moe_gather",
    )(slot_token, h)


def _combine_kernel(pos_ref, meta_ref, x_ref, g_ref, y_hbm, o_ref, rows_ref, sem, *, tc):
    def start(t, c):
        for k in range(TOP_K):
            _row_copy(y_hbm, pos_ref[TOP_K * t + k], rows_ref.at[k], t, sem).start()
        return c

    def wait(t, c):
        for k in range(TOP_K):
            _row_copy(y_hbm, pos_ref[TOP_K * t + k], rows_ref.at[k], t, sem).wait()
        return c

    lax.fori_loop(0, tc, start, 0)
    lax.fori_loop(0, tc, wait, 0)
    meta = meta_ref[...]
    y = meta[:, 2:3] * rows_ref[0] + meta[:, 3:4] * rows_ref[1]
    o_ref[...] = x_ref[...] + g_ref[0] * y


def moe_combine(yg, pos_flat, meta, x, mod, gate_idx, seq):
    n, d = x.shape
    tc = 256
    per_seq = seq // tc
    return pl.pallas_call(
        functools.partial(_combine_kernel, tc=tc),
        grid=(n // tc,),
        in_specs=[pl.BlockSpec((TOP_K * tc,), lambda i: (i,), memory_space=pltpu.SMEM),
                  pl.BlockSpec((tc, LANES), lambda i: (i, 0)),
                  pl.BlockSpec((tc, d), lambda i: (i, 0)),
                  pl.BlockSpec((1, 1, d), lambda i: ((i // per_seq) * N_ADA + gate_idx, 0, 0)),
                  pl.BlockSpec(memory_space=pl.ANY)],
        out_specs=pl.BlockSpec((tc, d), lambda i: (i, 0)),
        out_shape=jax.ShapeDtypeStruct((n, d), F32),
        scratch_shapes=[pltpu.VMEM((TOP_K, tc, d), F32), pltpu.SemaphoreType.DMA(())],
        compiler_params=_params(("arbitrary",), 32),
        name="moe_combine",
    )(pos_flat, meta, x, mod, yg)


def moe_ffn(h, router, w_gu, w_down, x, mod, gate_idx, seq):
    n, d = h.shape
    meta, counts = moe_route(h, router)
    cnt = counts[0, :N_EXPERTS].astype(I32)
    tiles_e = (cnt + MOE_TILE - 1) // MOE_TILE
    tile_end = jnp.cumsum(tiles_e)
    offset = (tile_end - tiles_e) * MOE_TILE
    n_tiles = (n * TOP_K) // MOE_TILE + N_EXPERTS
    tile_expert = jnp.minimum(
        jnp.sum(jnp.arange(n_tiles, dtype=I32)[:, None] >= tile_end[None, :], axis=1), N_EXPERTS - 1).astype(I32)
    n_used = tile_end[-1:].astype(I32)
    tile_expert = jnp.where(jnp.arange(n_tiles) < n_used[0], tile_expert, tile_expert[jnp.maximum(n_used[0] - 1, 0)])
    expert = meta[:, 0:TOP_K].astype(I32)
    rank = meta[:, 4:4 + TOP_K].astype(I32)
    pos_flat = (offset[expert] + rank).reshape(-1)
    token = jnp.repeat(jnp.arange(n, dtype=I32), TOP_K)
    slot_token = jnp.zeros((n_tiles * MOE_TILE,), I32).at[pos_flat].set(token)
    hg = moe_gather(h, slot_token)
    yg = ffn_grouped(hg, w_gu, w_down, tile_expert, n_used)
    return moe_combine(yg, pos_flat, meta, x, mod, gate_idx, seq)


def _gelu_tanh(x):
    return 0.5 * x * (1.0 + jnp.tanh(0.7978845608028654 * (x + 0.044715 * x * x * x)))


def _nsa_prep_kernel(kc_ref, vc_ref, ks_ref, vs_ref, kw_ref, vw_ref, kg_ref, pos_ref, w1_ref, w2_ref,
                     kso_ref, vso_ref, kwo_ref, vwo_ref, kco_ref, vco_ref, *, nc):
    kso_ref[0, 0] = _rms(ks_ref[0], kg_ref[1:2, :]).astype(BF16)
    kwo_ref[0, 0] = _rms(kw_ref[0], kg_ref[2:3, :]).astype(BF16)
    vso_ref[0, 0] = vs_ref[0].astype(BF16)
    vwo_ref[0, 0] = vw_ref[0].astype(BF16)

    half = NSA_CMP_BLOCK // 2

    def compress(src_ref, j):
        first = jnp.zeros((nc, NSA_DH), F32)
        second = jnp.zeros((nc, NSA_DH), F32)
        for l in range(half):
            rows = src_ref[0, pl.ds(l, nc, stride=NSA_CMP_STRIDE), :]
            first += jnp.dot((rows + pos_ref[j, l:l + 1, :]).astype(BF16), w1_ref[j, l].astype(BF16),
                             preferred_element_type=F32)
            second += jnp.dot((rows + pos_ref[j, half + l:half + l + 1, :]).astype(BF16),
                              w1_ref[j, half + l].astype(BF16), preferred_element_type=F32)
        pre = first + pltpu.roll(second, nc - 1, axis=0)
        hid = _gelu_tanh(pre)
        return jnp.dot(hid.astype(BF16), w2_ref[j].astype(BF16), preferred_element_type=F32)

    kco_ref[0, 0] = _rms(compress(kc_ref, 0), kg_ref[0:1, :]).astype(BF16)
    vco_ref[0, 0] = compress(vc_ref, 1).astype(BF16)


def nsa_prep(proj, k_gain, cmp_pos, cmp_w1, cmp_w2):
    b, t, _ = proj.shape
    g, dh = NSA_GROUPS, NSA_DH
    nc = t // NSA_CMP_STRIDE
    qblocks = NSA_HEADS

    def col(j):
        return pl.BlockSpec((1, t, dh), lambda i, gg: (i, 0, qblocks + j * g + gg))

    full = lambda shape: pl.BlockSpec(shape, lambda i, gg: (0,) * len(shape))
    seq_out = pl.BlockSpec((1, 1, t, dh), lambda i, gg: (i, gg, 0, 0))
    cmp_out = pl.BlockSpec((1, 1, nc, dh), lambda i, gg: (i, gg, 0, 0))
    return pl.pallas_call(
        functools.partial(_nsa_prep_kernel, nc=nc),
        grid=(b, g),
        in_specs=[col(0), col(1), col(2), col(3), col(4), col(5),
                  full((3, dh)), full((2, NSA_CMP_BLOCK, dh)), full((2, NSA_CMP_BLOCK, dh, dh)), full((2, dh, dh))],
        out_specs=[seq_out, seq_out, seq_out, seq_out, cmp_out, cmp_out],
        out_shape=[jax.ShapeDtypeStruct((b, g, t, dh), BF16)] * 4 + [jax.ShapeDtypeStruct((b, g, nc, dh), BF16)] * 2,
        compiler_params=_params(("parallel", "parallel"), 48),
        name="nsa_prep",
    )(proj, proj, proj, proj, proj, proj, k_gain, cmp_pos, cmp_w1, cmp_w2)


def _online_softmax_step(carry, s_heads, ok, v, tq):
    m, l, acc = carry
    s = jnp.concatenate([jnp.where(ok, sh, NEG_BIG) for sh in s_heads], axis=0)
    m_new = jnp.maximum(m, jnp.max(s, axis=-1, keepdims=True))
    alpha = jnp.exp(m - m_new)
    p = jnp.exp(s - m_new)
    l = alpha * l + jnp.sum(p, axis=-1, keepdims=True)
    acc = alpha * acc + jnp.dot(p.astype(BF16), v, preferred_element_type=F32)
    return m_new, l, acc


def _nsa_attn_kernel(slope_ref, q_ref, gl_ref, ks_ref, vs_ref, kw_ref, vw_ref, kc_ref, vc_ref, qg_ref,
                     o_ref, selx_ref, *, tq, ck, t_len, nc, nbl):
    grp = pl.program_id(1)
    t0 = pl.program_id(2) * tq
    rep, dh = NSA_REP, NSA_DH
    scale = dh ** -0.5
    slopes = [slope_ref[grp * rep + r] for r in range(rep)]

    q = q_ref[0]
    qn = jnp.concatenate([_rms(q[:, r * dh:(r + 1) * dh], qg_ref[...]) for r in range(rep)], axis=0)
    qb = (qn * scale).astype(BF16)
    tpos = t0 + lax.broadcasted_iota(I32, (tq, 1), 0)

    cmp_end = lax.broadcasted_iota(I32, (1, nc), 1) * NSA_CMP_STRIDE + (NSA_CMP_BLOCK - 1)
    dist_c = (tpos - cmp_end).astype(F32)
    ok_c = dist_c >= 0
    s_c = lax.dot_general(qb, kc_ref[0, 0], _NT, preferred_element_type=F32)
    p_heads = []
    for r in range(rep):
        s = jnp.where(ok_c, s_c[r * tq:(r + 1) * tq] - slopes[r] * dist_c, NEG_BIG)
        m = jnp.max(s, axis=-1, keepdims=True)
        e = jnp.where(ok_c, jnp.exp(s - m), 0.0)
        p_heads.append(e / jnp.maximum(jnp.sum(e, axis=-1, keepdims=True), 1e-30))
    o_cmp = jnp.dot(jnp.concatenate(p_heads, axis=0).astype(BF16), vc_ref[0, 0], preferred_element_type=F32)

    p_sum = p_heads[0]
    for r in range(1, rep):
        p_sum = p_sum + p_heads[r]
    cstart = lax.broadcasted_iota(I32, (nc, nbl), 0) * NSA_CMP_STRIDE
    bstart = lax.broadcasted_iota(I32, (nc, nbl), 1) * NSA_SEL_BLOCK
    overlap = jnp.where((cstart < bstart + NSA_SEL_BLOCK) & (cstart + NSA_CMP_BLOCK > bstart), 1.0, 0.0).astype(BF16)
    imp = jnp.zeros((tq, nbl), F32)
    for part in _split3(p_sum):
        imp += jnp.dot(part, overlap, preferred_element_type=F32)
    blk = lax.broadcasted_iota(I32, (tq, nbl), 1)
    cur = tpos >> 6
    forced = (blk == 0) | (blk == cur) | (blk == cur - 1)
    future = blk * NSA_SEL_BLOCK > tpos
    score = jnp.where(future, -SEL_FORCE, jnp.where(forced, SEL_FORCE, imp))
    blk_f = blk.astype(F32)
    sel = jnp.zeros((tq, nbl), F32)
    taken = jnp.float32(-3e38)
    for _ in range(min(NSA_TOPK, t_len // NSA_SEL_BLOCK)):
        best = jnp.max(score, axis=-1, keepdims=True)
        first = jnp.min(jnp.where(score == best, blk_f, float(nbl)), axis=-1, keepdims=True)
        pick = blk_f == first
        sel = jnp.where(pick, 1.0, sel)
        score = jnp.where(pick, taken, score)
    eblk = lax.broadcasted_iota(I32, (nbl, t_len), 0)
    ekey = lax.broadcasted_iota(I32, (nbl, t_len), 1) >> 6
    expand = jnp.where(eblk == ekey, 1.0, 0.0).astype(BF16)
    selx_ref[...] = jnp.dot(sel.astype(BF16), expand, preferred_element_type=F32)

    rows = rep * tq
    init = (jnp.full((rows, 1), NEG_BIG, F32), jnp.zeros((rows, 1), F32), jnp.zeros((rows, dh), F32))

    def sel_chunk(c, carry):
        k0 = pl.multiple_of(c * ck, ck)
        k = ks_ref[0, 0, pl.ds(k0, ck), :]
        v = vs_ref[0, 0, pl.ds(k0, ck), :]
        dist = (tpos - (k0 + lax.broadcasted_iota(I32, (1, ck), 1))).astype(F32)
        ok = (selx_ref[:, pl.ds(k0, ck)] > 0.5) & (dist >= 0)
        s_all = lax.dot_general(qb, k, _NT, preferred_element_type=F32)
        s_heads = [s_all[r * tq:(r + 1) * tq] - slopes[r] * dist for r in range(rep)]
        return _online_softmax_step(carry, s_heads, ok, v, tq)

    n_chunks = (t0 + tq + ck - 1) // ck
    _, l_s, acc_s = lax.fori_loop(0, n_chunks, sel_chunk, init)
    o_slc = acc_s / jnp.maximum(l_s, 1e-30)

    carry = init
    for w in range(NSA_WINDOW // tq, -1, -1):
        kpos0 = t0 - w * tq
        k0 = pl.multiple_of(jnp.maximum(kpos0, 0), tq)
        k = kw_ref[0, 0, pl.ds(k0, tq), :]
        v = vw_ref[0, 0, pl.ds(k0, tq), :]
        kpos = kpos0 + lax.broadcasted_iota(I32, (1, tq), 1)
        dist_i = tpos - kpos
        dist = dist_i.astype(F32)
        ok = (dist_i >= 0) & (dist_i < NSA_WINDOW) & (kpos >= 0)
        s_all = lax.dot_general(qb, k, _NT, preferred_element_type=F32)
        s_heads = [s_all[r * tq:(r + 1) * tq] - slopes[r] * dist for r in range(rep)]
        carry = _online_softmax_step(carry, s_heads, ok, v, tq)
    o_win = carry[2] / jnp.maximum(carry[1], 1e-30)

    gates = jax.nn.sigmoid(gl_ref[0])
    lane = lax.broadcasted_iota(I32, (tq, LANES), 1)

    def gate(r, j):
        return jnp.sum(jnp.where(lane == 3 * r + j, gates, 0.0), axis=-1, keepdims=True)

    for r in range(rep):
        sl = slice(r * tq, (r + 1) * tq)
        o = gate(r, 0) * o_cmp[sl] + gate(r, 1) * o_slc[sl] + gate(r, 2) * o_win[sl]
        o_ref[0, :, r * dh:(r + 1) * dh] = o.astype(o_ref.dtype)


def nsa_attention(proj, ksn, vs, kwn, vw, kcmp, vcmp, q_gain, slopes):
    b, t, _ = proj.shape
    g, rep, dh = NSA_GROUPS, NSA_REP, NSA_DH
    tq = 128
    ck = min(512, t)
    nc = t // NSA_CMP_STRIDE
    nbl = max(LANES, t // NSA_SEL_BLOCK)
    gate_block = (NSA_HEADS + 6 * g)
    seq_in = pl.BlockSpec((1, 1, t, dh), lambda i, gg, qi, sl: (i, gg, 0, 0))
    cmp_in = pl.BlockSpec((1, 1, nc, dh), lambda i, gg, qi, sl: (i, gg, 0, 0))
    grid_spec = pltpu.PrefetchScalarGridSpec(
        num_scalar_prefetch=1,
        grid=(b, g, t // tq),
        in_specs=[pl.BlockSpec((1, tq, rep * dh), lambda i, gg, qi, sl: (i, qi, gg)),
                  pl.BlockSpec((1, tq, LANES), lambda i, gg, qi, sl: (i, qi, gate_block + gg)),
                  seq_in, seq_in, seq_in, seq_in, cmp_in, cmp_in,
                  pl.BlockSpec((1, dh), lambda i, gg, qi, sl: (0, 0))],
        out_specs=pl.BlockSpec((1, tq, rep * dh), lambda i, gg, qi, sl: (i, qi, gg)),
        scratch_shapes=[pltpu.VMEM((tq, t), F32)],
    )
    return pl.pallas_call(
        functools.partial(_nsa_attn_kernel, tq=tq, ck=ck, t_len=t, nc=nc, nbl=nbl),
        grid_spec=grid_spec,
        out_shape=jax.ShapeDtypeStruct((b, t, g * rep * dh), BF16),
        compiler_params=_params(("parallel", "parallel", "arbitrary"), 48),
        name="nsa_attn",
    )(slopes, proj, proj, ksn, vs, kwn, vw, kcmp, vcmp, q_gain.reshape(1, dh))


def nsa_mixer(h, w_in, w_out, q_gain, k_gain, cmp_pos, cmp_w1, cmp_w2, x, mod, batch, seq):
    d = h.shape[1]
    g, rep, dh = NSA_GROUPS, NSA_REP, NSA_DH
    n_main = NSA_HEADS * dh + 6 * g * dh
    w_gate = w_in[:, n_main:].reshape(d, g, rep * 3)
    w_gate = jnp.pad(w_gate, ((0, 0), (0, 0), (0, LANES - rep * 3))).reshape(d, g * LANES)
    w_all = jnp.concatenate([w_in[:, :n_main], w_gate], axis=1).astype(BF16)
    proj = matmul(h, w_all, F32).reshape(batch, seq, -1)
    ksn, vs, kwn, vw, kcmp, vcmp = nsa_prep(proj, k_gain, cmp_pos, cmp_w1, cmp_w2)
    slopes = jnp.exp2(-8.0 * jnp.arange(1, NSA_HEADS + 1, dtype=F32) / NSA_HEADS)
    o = nsa_attention(proj, ksn, vs, kwn, vw, kcmp, vcmp, q_gain, slopes)
    return matmul_residual(o.reshape(batch * seq, -1), w_out.astype(BF16), x, mod, 2, seq)


def _hgrn_kernel(q_ref, f_ref, v_ref, g_ref, lb_ref, og_ref, o_ref, *, t_len, chunk):
    blk = HGRN_BLOCK
    dk = q_ref.shape[-1]
    lb = lb_ref[0]
    log_lb = jnp.log(lb)
    log_1m_lb = jnp.log1p(-lb)
    row_in_blk = lax.broadcasted_iota(I32, (chunk, dk), 0) & (blk - 1)

    def shift_rows(x, s):
        return pltpu.roll(x, s, axis=0)

    def chunk_step(c, state_t):
        r0 = pl.multiple_of(c * chunk, chunk)
        q = _silu(q_ref[0, pl.ds(r0, chunk), :])
        z = f_ref[0, pl.ds(r0, chunk), :]
        v = v_ref[0, pl.ds(r0, chunk), :]
        log_sig = jnp.minimum(z, 0.0) - jnp.log1p(jnp.exp(-jnp.abs(z)))
        a2 = log_1m_lb + log_sig
        hi = jnp.maximum(log_lb, a2)
        logf = hi + jnp.log1p(jnp.exp(-jnp.abs(log_lb - a2)))
        k = (1.0 - lb) * jax.nn.sigmoid(-z)
        bsum = logf
        s = 1
        while s < blk:
            bsum = bsum + jnp.where(row_in_blk >= s, shift_rows(bsum, s), 0.0)
            s *= 2
        o = jnp.zeros((chunk, dk), F32)
        for dlt in range(blk):
            ok = row_in_blk >= dlt
            kd = shift_rows(k, dlt) if dlt else k
            bd = shift_rows(bsum, dlt) if dlt else bsum
            vd = shift_rows(v, dlt) if dlt else v
            w = q * kd * jnp.exp(jnp.where(ok, bsum - bd, 0.0))
            a = jnp.sum(jnp.where(ok, w, 0.0), axis=-1, keepdims=True)
            o = o + a * vd
        outs = []
        for i in range(chunk // blk):
            sl = slice(i * blk, (i + 1) * blk)
            b_i = bsum[sl]
            b_last = b_i[blk - 1:blk, :]
            q_hat = (q[sl] * jnp.exp(b_i)).astype(BF16)
            outs.append(o[sl] + lax.dot_general(q_hat, state_t.astype(BF16), _NT, preferred_element_type=F32))
            k_hat = (k[sl] * jnp.exp(b_last - b_i)).astype(BF16)
            state_t = jnp.exp(b_last) * state_t + lax.dot_general(
                v[sl].astype(BF16), k_hat, _TN, preferred_element_type=F32)
        o_all = jnp.concatenate(outs, axis=0)
        gate = _silu(g_ref[0, pl.ds(r0, chunk), :])
        o_ref[0, pl.ds(r0, chunk), :] = (_rms(o_all, og_ref[...]) * gate).astype(o_ref.dtype)
        return state_t

    lax.fori_loop(0, t_len // chunk, chunk_step, jnp.zeros((dk, dk), F32))


def hgrn_recurrence(proj, lb, o_gain):
    b, t, _ = proj.shape
    hh = HGRN_HEADS
    dk = lb.shape[0] // hh
    chunk = 64

    def col(j):
        return pl.BlockSpec((1, t, dk), lambda i, h: (i, 0, j * hh + h))

    return pl.pallas_call(
        functools.partial(_hgrn_kernel, t_len=t, chunk=chunk),
        grid=(b, hh),
        in_specs=[col(0), col(1), col(2), col(3),
                  pl.BlockSpec((1, 1, dk), lambda i, h: (h, 0, 0)),
                  pl.BlockSpec((1, dk), lambda i, h: (0, 0))],
        out_specs=pl.BlockSpec((1, t, dk), lambda i, h: (i, 0, h)),
        out_shape=jax.ShapeDtypeStruct((b, t, hh * dk), BF16),
        compiler_params=_params(("parallel", "parallel"), 48),
        name="hgrn",
    )(proj, proj, proj, proj, lb.reshape(hh, 1, dk), o_gain.reshape(1, dk))


def hgrn2_mixer(h, w_in, w_out, o_gain, lb, x, mod, batch, seq):
    proj = matmul(h, w_in.astype(BF16), F32).reshape(batch, seq, -1)
    o = hgrn_recurrence(proj, lb, o_gain)
    return matmul_residual(o.reshape(batch * seq, -1), w_out.astype(BF16), x, mod, 2, seq)


def _swa_kernel(sink_ref, q_ref, kp_ref, kc_ref, vp_ref, vc_ref, qg_ref, kg_ref, o_ref, *, tq):
    t0 = pl.program_id(1) * tq
    dh, rep = SWA_DH, SWA_REP
    scale = dh ** -0.5
    q = q_ref[0]
    kcat = jnp.concatenate([kp_ref[0], kc_ref[0]], axis=0)
    vcat = jnp.concatenate([vp_ref[0], vc_ref[0]], axis=0)
    tpos = t0 + lax.broadcasted_iota(I32, (tq, 1), 0)
    kpos = t0 - tq + lax.broadcasted_iota(I32, (1, 2 * tq), 1)
    dist_i = tpos - kpos
    dist = dist_i.astype(F32)
    ok = (dist_i >= 0) & (dist_i < SWA_WINDOW) & (kpos >= 0)
    outs = []
    for g in range(SWA_GROUPS):
        k = _rms(kcat[:, g * dh:(g + 1) * dh], kg_ref[...]).astype(BF16)
        v = vcat[:, g * dh:(g + 1) * dh].astype(BF16)
        heads = [g * rep + r for r in range(rep)]
        qs = jnp.concatenate([_rms(q[:, hd * dh:(hd + 1) * dh], qg_ref[...]) for hd in heads], axis=0)
        s_all = lax.dot_general((qs * scale).astype(BF16), k, _NT, preferred_element_type=F32)
        ps = []
        for r, hd in enumerate(heads):
            slope = 2.0 ** (-8.0 * (hd + 1) / SWA_HEADS)
            sink = sink_ref[hd]
            s = jnp.where(ok, s_all[r * tq:(r + 1) * tq] - slope * dist, NEG_BIG)
            m = jnp.maximum(jnp.max(s, axis=-1, keepdims=True), sink)
            e = jnp.where(ok, jnp.exp(s - m), 0.0)
            den = jnp.sum(e, axis=-1, keepdims=True) + jnp.exp(sink - m)
            ps.append(e / jnp.maximum(den, 1e-30))
        o = jnp.dot(jnp.concatenate(ps, axis=0).astype(BF16), v, preferred_element_type=F32)
        outs += [o[r * tq:(r + 1) * tq] for r in range(rep)]
    o_ref[0] = jnp.concatenate(outs, axis=-1).astype(o_ref.dtype)


def swa_attention(proj, q_gain, k_gain, sinks):
    b, t, _ = proj.shape
    hd = SWA_HEADS * SWA_DH
    kv = SWA_GROUPS * SWA_DH
    tq = SWA_WINDOW
    kblk = hd // kv
    prev = lambda qi: jnp.maximum(qi - 1, 0)
    grid_spec = pltpu.PrefetchScalarGridSpec(
        num_scalar_prefetch=1,
        grid=(b, t // tq),
        in_specs=[pl.BlockSpec((1, tq, hd), lambda i, qi, sk: (i, qi, 0)),
                  pl.BlockSpec((1, tq, kv), lambda i, qi, sk: (i, prev(qi), kblk)),
                  pl.BlockSpec((1, tq, kv), lambda i, qi, sk: (i, qi, kblk)),
                  pl.BlockSpec((1, tq, kv), lambda i, qi, sk: (i, prev(qi), kblk + 1)),
                  pl.BlockSpec((1, tq, kv), lambda i, qi, sk: (i, qi, kblk + 1)),
                  pl.BlockSpec((1, SWA_DH), lambda i, qi, sk: (0, 0)),
                  pl.BlockSpec((1, SWA_DH), lambda i, qi, sk: (0, 0))],
        out_specs=pl.BlockSpec((1, tq, hd), lambda i, qi, sk: (i, qi, 0)),
    )
    return pl.pallas_call(
        functools.partial(_swa_kernel, tq=tq),
        grid_spec=grid_spec,
        out_shape=jax.ShapeDtypeStruct((b, t, hd), BF16),
        compiler_params=_params(("parallel", "parallel"), 40),
        name="swa",
    )(sinks, proj, proj, proj, proj, proj, q_gain.reshape(1, -1), k_gain.reshape(1, -1))


def swa_mixer(h, w_in, w_out, q_gain, k_gain, sinks, x, mod, batch, seq):
    proj = matmul(h, w_in.astype(BF16), F32).reshape(batch, seq, -1)
    o = swa_attention(proj, q_gain, k_gain, sinks)
    return matmul_residual(o.reshape(batch * seq, -1), w_out.astype(BF16), x, mod, 2, seq)


def kernel(x, c, norm_mix_g, norm_ffn_g, ada_w, ada_b, nsa_w_in, nsa_w_out, nsa_q_gain, nsa_k_gain,
           nsa_cmp_pos, nsa_cmp_w1, nsa_cmp_w2, hgrn_w_in, hgrn_w_out, hgrn_o_gain, hgrn_lb_logits,
           swa_w_in, swa_w_out, swa_q_gain, swa_k_gain, swa_sinks, ffn_w_gu, ffn_w_down,
           moe_router, moe_w_gu, moe_w_down):
    batch, seq, d = x.shape
    depth = ada_w.shape[0]
    n = batch * seq
    mod_all = ada_all(c, ada_w, ada_b)
    lb_all = jax.nn.softmax(hgrn_lb_logits.astype(F32), axis=0)
    n_a = n_b = n_c = n_d = n_m = 0
    x2 = x.reshape(n, d)
    for i in range(depth):
        mod = mod_all[i].reshape(batch * N_ADA, 1, d)
        h = norm_mod(x2.reshape(batch, seq, d), norm_mix_g[i], mod, 1, 0, BF16).reshape(n, d)
        kind = i % 3
        if kind == 0:
            x2 = nsa_mixer(h, nsa_w_in[n_a], nsa_w_out[n_a], nsa_q_gain[n_a], nsa_k_gain[n_a],
                           nsa_cmp_pos[n_a], nsa_cmp_w1[n_a], nsa_cmp_w2[n_a], x2, mod, batch, seq)
            n_a += 1
        elif kind == 1:
            lb = jnp.sum(lb_all[1:i + 1], axis=0)
            x2 = hgrn2_mixer(h, hgrn_w_in[n_b], hgrn_w_out[n_b], hgrn_o_gain[n_b], lb, x2, mod, batch, seq)
            n_b += 1
        else:
            x2 = swa_mixer(h, swa_w_in[n_c], swa_w_out[n_c], swa_q_gain[n_c], swa_k_gain[n_c], swa_sinks[n_c],
                           x2, mod, batch, seq)
            n_c += 1
        if i % 2 == 0:
            h = norm_mod(x2.reshape(batch, seq, d), norm_ffn_g[i], mod, 4, 3, BF16).reshape(n, d)
            x2 = ffn_dense(h, ffn_w_gu[n_d].astype(BF16), ffn_w_down[n_d].astype(BF16), x2, mod, 5, seq)
            n_d += 1
        else:
            h = norm_mod(x2.reshape(batch, seq, d), norm_ffn_g[i], mod, 4, 3, F32).reshape(n, d)
            x2 = moe_ffn(h, moe_router[n_m], moe_w_gu[n_m].astype(BF16), moe_w_down[n_m].astype(BF16),
                         x2, mod, 5, seq)
            n_m += 1
    return x2.reshape(batch, seq, d)
```
